```python
import jax, jax.numpy as jnp
from jax import lax
import numpy as np

D_MODEL = 2048
BATCH = 4
SEQ = 2048
DEPTH = 4
DEC_BATCH = 128
DEC_SEQ = 8
PAST_LEN = 8192
PAGE_SIZE = 128

N_EVEN = (DEPTH + 1) // 2
N_ODD = DEPTH // 2
MLA_HEADS = 8
MLA_NOPE = 128
MLA_ROPE = 64
MLA_V = 128
Q_LORA = 512
KV_LORA = 256
CONF_CH = 1024
CONF_K = 31
SC_CH = 1024
SC_K = 3
SWA_HEADS = 16
SWA_KV = 4
SWA_HD = 64
WINDOW = 128
MEM_TOKENS = 256
MEM_HEADS = 4
MEM_HD = 128
MEM_W = MEM_HEADS * MEM_HD
D_FF = 5632
FFN_K = 3
ROPE_THETA = 10000.0
EPS = 1e-6
Q_BLOCK = 128
NEG_INF = -1e30
MLA_SCALE = (MLA_NOPE + MLA_ROPE) ** -0.5
SWA_SCALE = SWA_HD ** -0.5
MEM_SCALE = MEM_HD ** -0.5
EVEN_IN = Q_LORA + KV_LORA + MLA_ROPE + 2 * CONF_CH
EVEN_OUT = MLA_HEADS * MLA_V + CONF_CH
ODD_IN = 3 * SC_CH + (SWA_HEADS + 2 * SWA_KV) * SWA_HD
ODD_OUT = SC_CH + SWA_HEADS * SWA_HD
N_PAGES = PAST_LEN // PAGE_SIZE
N_USED_PAGES = DEC_BATCH * N_PAGES
N_POOL = N_USED_PAGES + max(1, N_USED_PAGES // 4)
SWA_BUF = min(WINDOW, PAST_LEN)

kernel_name = "hybrid_mla_conformer_shortconv_swa_decoder_step"

STATE_NAMES = ("mla_ckv", "mla_kpe", "swa_k", "swa_v", "mem_k", "mem_v", "conf_conv", "sc_conv", "ffn_conv")


def rmsnorm(x, g):
    xf = x.astype(jnp.float32)
    y = xf * lax.rsqrt(jnp.mean(xf * xf, axis=-1, keepdims=True) + EPS)
    return (y * g.astype(jnp.float32)).astype(x.dtype)


def layernorm(x, g, b):
    xf = x.astype(jnp.float32)
    mu = jnp.mean(xf, axis=-1, keepdims=True)
    var = jnp.mean(jnp.square(xf - mu), axis=-1, keepdims=True)
    y = (xf - mu) * lax.rsqrt(var + EPS)
    return (y * g.astype(jnp.float32) + b.astype(jnp.float32)).astype(x.dtype)


def rope_tables(pos, dim):
    inv = ROPE_THETA ** (-jnp.arange(0, dim, 2, dtype=jnp.float32) / dim)
    ang = pos.astype(jnp.float32)[:, None] * inv[None, :]
    return jnp.cos(ang), jnp.sin(ang)


def apply_rope(x, cos, sin):
    half = x.shape[-1] // 2
    c = cos[:, None, :].astype(x.dtype)
    s = sin[:, None, :].astype(x.dtype)
    x1, x2 = x[..., :half], x[..., half:]
    return jnp.concatenate([x1 * c - x2 * s, x2 * c + x1 * s], axis=-1)


def causal_dwconv(buf, x, w):
    xcat = jnp.concatenate([buf, x], axis=1)
    out = lax.conv_general_dilated(
        xcat, w[:, None, :].astype(x.dtype), window_strides=(1,), padding="VALID",
        dimension_numbers=("NWC", "WIO", "NWC"), feature_group_count=x.shape[-1])
    return out, xcat[:, -(w.shape[0] - 1):]


def softmax_with_sink(s, sink):
    sink = jnp.broadcast_to(sink.astype(jnp.float32), s.shape[:-1] + (1,))
    return jax.nn.softmax(jnp.concatenate([s, sink], axis=-1), axis=-1)[..., :-1]


def gather_pages(pool, page_table):
    g = jnp.take(pool, page_table, axis=0)
    return g.reshape(g.shape[0], -1, g.shape[-1])


def mla_attention(q_lat, q_pe, q_pos, parts):
    B, S, H, R = q_lat.shape
    qb = min(S, Q_BLOCK)
    nb = S // qb

    def block(args):
        ql, qp, qpos = args
        scores = []
        for ckv, kpe, kpos in parts:
            s = jnp.einsum("bqhr,bkr->bhqk", ql, ckv) + jnp.einsum("bqhp,bkp->bhqk", qp, kpe)
            s = s.astype(jnp.float32) * MLA_SCALE
            scores.append(jnp.where(kpos[None, :] <= qpos[:, None], s, NEG_INF))
        prob = jax.nn.softmax(jnp.concatenate(scores, axis=-1), axis=-1).astype(ql.dtype)
        outs = []
        off = 0
        for ckv, _, kpos in parts:
            n = kpos.shape[0]
            outs.append(jnp.einsum("bhqk,bkr->bqhr", prob[..., off:off + n], ckv))
            off += n
        return sum(outs[1:], outs[0])

    blocks = (q_lat.reshape(B, nb, qb, H, R).swapaxes(0, 1),
              q_pe.reshape(B, nb, qb, H, q_pe.shape[-1]).swapaxes(0, 1),
              q_pos.reshape(nb, qb))
    o = lax.map(block, blocks)
    return o.swapaxes(0, 1).reshape(B, S, H, R)


def swa_banded(q, k, v, sinks):
    B, S, H, D = q.shape
    G = H // SWA_KV
    W = WINDOW
    nb = S // W
    qr = q.reshape(B, nb, W, SWA_KV, G, D)
    pad = jnp.zeros((B, W, SWA_KV, D), k.dtype)
    kp = jnp.concatenate([pad, k], axis=1).reshape(B, nb + 1, W, SWA_KV, D)
    vp = jnp.concatenate([pad, v], axis=1).reshape(B, nb + 1, W, SWA_KV, D)
    kb = jnp.concatenate([kp[:, :-1], kp[:, 1:]], axis=2)
    vb = jnp.concatenate([vp[:, :-1], vp[:, 1:]], axis=2)
    s = jnp.einsum("bnqkgd,bnskd->bnkgqs", qr, kb).astype(jnp.float32) * SWA_SCALE
    qi = jnp.arange(W)[None, :, None]
    si = jnp.arange(2 * W)[None, None, :]
    blk = jnp.arange(nb)[:, None, None]
    rel = qi + W - si
    valid = (rel >= 0) & (rel < W) & (blk * W + si - W >= 0)
    s = jnp.where(valid[None, :, None, None], s, NEG_INF)
    p = softmax_with_sink(s, sinks[None, None, :, :, None, None]).astype(v.dtype)
    o = jnp.einsum("bnkgqs,bnskd->bnqkgd", p, vb)
    return o.reshape(B, S, H * D)


def swa_buffered(q, kk, vv, q_pos, sinks):
    B, S, H, D = q.shape
    G = H // SWA_KV
    n_keys = kk.shape[1]
    k_pos = q_pos[0] - (n_keys - S) + jnp.arange(n_keys, dtype=jnp.int32)
    qr = q.reshape(B, S, SWA_KV, G, D)
    s = jnp.einsum("bqkgd,bskd->bkgqs", qr, kk).astype(jnp.float32) * SWA_SCALE
    rel = q_pos[:, None] - k_pos[None, :]
    valid = (rel >= 0) & (rel < WINDOW) & (k_pos[None, :] >= 0)
    s = jnp.where(valid[None, None, None], s, NEG_INF)
    p = softmax_with_sink(s, sinks[None, :, :, None, None]).astype(vv.dtype)
    o = jnp.einsum("bkgqs,bskd->bqkgd", p, vv)
    return o.reshape(B, S, H * D)


def even_mixer(h, pos, conf_buf, kv_past, p, i):
    B, S, _ = h.shape
    z = h @ p["w_even_in"][i]
    q_a, ckv, kpe, glu = jnp.split(z, [Q_LORA, Q_LORA + KV_LORA, Q_LORA + KV_LORA + MLA_ROPE], axis=-1)
    q = (rmsnorm(q_a, p["g_q_a"][i]) @ p["w_q_b"][i]).reshape(B, S, MLA_HEADS, MLA_NOPE + MLA_ROPE)
    cos, sin = rope_tables(pos, MLA_ROPE)
    q_nope = q[..., :MLA_NOPE]
    q_pe = apply_rope(q[..., MLA_NOPE:], cos, sin)
    ckv = rmsnorm(ckv, p["g_kv_a"][i])
    kpe = apply_rope(kpe[:, :, None, :], cos, sin)[:, :, 0]
    q_lat = jnp.einsum("bshn,rhn->bshr", q_nope, p["w_uk"][i])
    parts = [(ckv, kpe, pos)]
    if kv_past is not None:
        ckv_p, kpe_p = kv_past
        parts = [(ckv_p, kpe_p, jnp.arange(ckv_p.shape[1], dtype=jnp.int32)), (ckv, kpe, pos)]
    o_lat = mla_attention(q_lat, q_pe, pos, parts)
    y_mla = jnp.einsum("bshr,rhv->bshv", o_lat, p["w_uv"][i]).reshape(B, S, MLA_HEADS * MLA_V)
    a, gate = jnp.split(glu, 2, axis=-1)
    u = a * jax.nn.sigmoid(gate)
    c, conf_new = causal_dwconv(conf_buf, u, p["conf_dw_w"][i])
    c = jax.nn.silu(layernorm(c + p["conf_dw_b"][i], p["conf_ln_g"][i], p["conf_ln_b"][i]))
    out = jnp.concatenate([y_mla, c], axis=-1) @ p["w_even_out"][i]
    return out, ckv, kpe, conf_new


def odd_mixer(h, pos, sc_buf, swa_past, p, j):
    B, S, _ = h.shape
    z = h @ p["w_odd_in"][j]
    cuts = [SC_CH, 2 * SC_CH, 3 * SC_CH, 3 * SC_CH + SWA_HEADS * SWA_HD,
            3 * SC_CH + (SWA_HEADS + SWA_KV) * SWA_HD]
    gb, gc, xs, q, k, v = jnp.split(z, cuts, axis=-1)
    conv, sc_new = causal_dwconv(sc_buf, gc * xs, p["sc_conv_w"][j])
    y_sc = gb * conv
    cos, sin = rope_tables(pos, SWA_HD)
    q = apply_rope(q.reshape(B, S, SWA_HEADS, SWA_HD), cos, sin)
    k = apply_rope(k.reshape(B, S, SWA_KV, SWA_HD), cos, sin)
    v = v.reshape(B, S, SWA_KV, SWA_HD)
    sinks = p["swa_sinks"][j].reshape(SWA_KV, SWA_HEADS // SWA_KV)
    if swa_past is None:
        y_att = swa_banded(q, k, v, sinks)
        k_all, v_all = k, v
    else:
        buf_k, buf_v = swa_past
        k_all = jnp.concatenate([buf_k, k], axis=1)
        v_all = jnp.concatenate([buf_v, v], axis=1)
        y_att = swa_buffered(q, k_all, v_all, pos, sinks)
    front = max(0, SWA_BUF - k_all.shape[1])
    k_all = jnp.pad(k_all, ((0, 0), (front, 0), (0, 0), (0, 0)))
    v_all = jnp.pad(v_all, ((0, 0), (front, 0), (0, 0), (0, 0)))
    out = jnp.concatenate([y_sc, y_att], axis=-1) @ p["w_odd_out"][j]
    return out, k_all[:, -SWA_BUF:], v_all[:, -SWA_BUF:], sc_new


def mem_attention(h, mk, mv, wq, wo):
    B, S, _ = h.shape
    q = (h @ wq).reshape(B, S, MEM_HEADS, MEM_HD)
    s = jnp.einsum("bqhd,bmhd->bhqm", q, mk).astype(jnp.float32) * MEM_SCALE
    prob = jax.nn.softmax(s, axis=-1).astype(mv.dtype)
    o = jnp.einsum("bhqm,bmhd->bqhd", prob, mv).reshape(B, S, MEM_W)
    return o @ wo


def conv_ffn(h, buf, p, layer):
    u = h @ p["w_ffn_up"][layer]
    c, ffn_new = causal_dwconv(buf, u, p["ffn_conv_w"][layer])
    gate, up = jnp.split(c + p["ffn_conv_b"][layer], 2, axis=-1)
    return (jax.nn.silu(gate) * up) @ p["w_ffn_down"][layer], ffn_new


def trunk(x, pos, p, mem, past, page_table):
    B = x.shape[0]
    new = {name: [] for name in STATE_NAMES}
    for layer in range(DEPTH):
        g = p["norm_gains"][layer]
        h = rmsnorm(x, g[0])
        if layer % 2 == 0:
            i = layer // 2
            if past is None:
                conf_buf = jnp.zeros((B, CONF_K - 1, CONF_CH), x.dtype)
                kv_past = None
            else:
                conf_buf = past["conf_conv"][i]
                kv_past = (gather_pages(past["mla_ckv"][i], page_table),
                           gather_pages(past["mla_kpe"][i], page_table))
            mix, ckv, kpe, conf_new = even_mixer(h, pos, conf_buf, kv_past, p, i)
            new["mla_ckv"].append(ckv)
            new["mla_kpe"].append(kpe)
            new["conf_conv"].append(conf_new)
        else:
            j = layer // 2
            if past is None:
                sc_buf = jnp.zeros((B, SC_K - 1, SC_CH), x.dtype)
                swa_past = None
            else:
                sc_buf = past["sc_conv"][j]
                swa_past = (past["swa_k"][j], past["swa_v"][j])
            mix, nk, nv, sc_new = odd_mixer(h, pos, sc_buf, swa_past, p, j)
            new["swa_k"].append(nk)
            new["swa_v"].append(nv)
            new["sc_conv"].append(sc_new)
        x = x + rmsnorm(mix, g[1])
        if past is None:
            kv = (mem @ p["w_mem_kv"][layer]).reshape(B, mem.shape[1], 2, MEM_HEADS, MEM_HD)
            mk, mv = kv[:, :, 0], kv[:, :, 1]
            new["mem_k"].append(mk)
            new["mem_v"].append(mv)
        else:
            mk, mv = past["mem_k"][layer], past["mem_v"][layer]
        x = x + rmsnorm(mem_attention(rmsnorm(x, g[2]), mk, mv, p["w_mem_q"][layer], p["w_mem_o"][layer]), g[3])
        ffn_buf = jnp.zeros((B, FFN_K - 1, 2 * D_FF), x.dtype) if past is None else past["ffn_conv"][layer]
        f, ffn_new = conv_ffn(rmsnorm(x, g[4]), ffn_buf, p, layer)
        new["ffn_conv"].append(ffn_new)
        x = x + rmsnorm(f, g[5])
    return x, {name: jnp.stack(v) for name, v in new.items() if v}


def setup_inputs(seed: int = 0) -> dict:
    key = jax.random.key(seed)
    ks = iter(jax.random.split(key, 64))

    def nrm(shape, scale=1.0):
        return jax.random.normal(next(ks), shape, jnp.float32) * scale

    def gain(shape):
        return 1.0 + nrm(shape, 0.05)

    perm = jax.random.permutation(next(ks), N_POOL)
    page_table = perm[:N_USED_PAGES].reshape(DEC_BATCH, N_PAGES).astype(jnp.int32)
    return {
        "x_prompt": nrm((BATCH, SEQ, D_MODEL)),
        "x_sample": nrm((DEC_BATCH, DEC_SEQ, D_MODEL)),
        "cache_mla_ckv": nrm((N_EVEN, N_POOL, PAGE_SIZE, KV_LORA)),
        "cache_mla_kpe": nrm((N_EVEN, N_POOL, PAGE_SIZE, MLA_ROPE)),
        "cache_swa_k": nrm((N_ODD, DEC_BATCH, SWA_BUF, SWA_KV, SWA_HD)),
        "cache_swa_v": nrm((N_ODD, DEC_BATCH, SWA_BUF, SWA_KV, SWA_HD)),
        "cache_mem_k": nrm((DEPTH, DEC_BATCH, MEM_TOKENS, MEM_HEADS, MEM_HD)),
        "cache_mem_v": nrm((DEPTH, DEC_BATCH, MEM_TOKENS, MEM_HEADS, MEM_HD)),
        "state_conf_conv": nrm((N_EVEN, DEC_BATCH, CONF_K - 1, CONF_CH)),
        "state_sc_conv": nrm((N_ODD, DEC_BATCH, SC_K - 1, SC_CH)),
        "state_ffn_conv": nrm((DEPTH, DEC_BATCH, FFN_K - 1, 2 * D_FF)),
        "page_table": page_table,
        "mem_prompt": nrm((BATCH, MEM_TOKENS, D_MODEL)),
        "norm_gains": gain((DEPTH, 6, D_MODEL)),
        "w_even_in": nrm((N_EVEN, D_MODEL, EVEN_IN), D_MODEL ** -0.5),
        "g_q_a": gain((N_EVEN, Q_LORA)),
        "w_q_b": nrm((N_EVEN, Q_LORA, MLA_HEADS * (MLA_NOPE + MLA_ROPE)), Q_LORA ** -0.5),
        "g_kv_a": gain((N_EVEN, KV_LORA)),
        "w_uk": nrm((N_EVEN, KV_LORA, MLA_HEADS, MLA_NOPE), KV_LORA ** -0.5),
        "w_uv": nrm((N_EVEN, KV_LORA, MLA_HEADS, MLA_V), KV_LORA ** -0.5),
        "conf_dw_w": nrm((N_EVEN, CONF_K, CONF_CH), CONF_K ** -0.5),
        "conf_dw_b": nrm((N_EVEN, CONF_CH), 0.01),
        "conf_ln_g": gain((N_EVEN, CONF_CH)),
        "conf_ln_b": nrm((N_EVEN, CONF_CH), 0.01),
        "w_even_out": nrm((N_EVEN, EVEN_OUT, D_MODEL), EVEN_OUT ** -0.5),
        "w_odd_in": nrm((N_ODD, D_MODEL, ODD_IN), D_MODEL ** -0.5),
        "sc_conv_w": nrm((N_ODD, SC_K, SC_CH), SC_K ** -0.5),
        "swa_sinks": nrm((N_ODD, SWA_HEADS), 0.5),
        "w_odd_out": nrm((N_ODD, ODD_OUT, D_MODEL), ODD_OUT ** -0.5),
        "w_mem_q": nrm((DEPTH, D_MODEL, MEM_W), D_MODEL ** -0.5),
        "w_mem_kv": nrm((DEPTH, D_MODEL, 2 * MEM_W), D_MODEL ** -0.5),
        "w_mem_o": nrm((DEPTH, MEM_W, D_MODEL), MEM_W ** -0.5),
        "w_ffn_up": nrm((DEPTH, D_MODEL, 2 * D_FF), D_MODEL ** -0.5),
        "ffn_conv_w": nrm((DEPTH, FFN_K, 2 * D_FF), FFN_K ** -0.5),
        "ffn_conv_b": nrm((DEPTH, 2 * D_FF), 0.01),
        "w_ffn_down": nrm((DEPTH, D_FF, D_MODEL), D_FF ** -0.5),
    }


def reference(x_prompt, x_sample, cache_mla_ckv, cache_mla_kpe, cache_swa_k, cache_swa_v,
              cache_mem_k, cache_mem_v, state_conf_conv, state_sc_conv, state_ffn_conv, page_table,
              mem_prompt, norm_gains, w_even_in, g_q_a, w_q_b, g_kv_a, w_uk, w_uv, conf_dw_w, conf_dw_b,
              conf_ln_g, conf_ln_b, w_even_out, w_odd_in, sc_conv_w, swa_sinks, w_odd_out, w_mem_q,
              w_mem_kv, w_mem_o, w_ffn_up, ffn_conv_w, ffn_conv_b, w_ffn_down):
    p = dict(norm_gains=norm_gains, w_even_in=w_even_in, g_q_a=g_q_a, w_q_b=w_q_b, g_kv_a=g_kv_a,
             w_uk=w_uk, w_uv=w_uv, conf_dw_w=conf_dw_w, conf_dw_b=conf_dw_b, conf_ln_g=conf_ln_g,
             conf_ln_b=conf_ln_b, w_even_out=w_even_out, w_odd_in=w_odd_in, sc_conv_w=sc_conv_w,
             swa_sinks=swa_sinks, w_odd_out=w_odd_out, w_mem_q=w_mem_q, w_mem_kv=w_mem_kv,
             w_mem_o=w_mem_o, w_ffn_up=w_ffn_up, ffn_conv_w=ffn_conv_w, ffn_conv_b=ffn_conv_b,
             w_ffn_down=w_ffn_down)
    pos_p = jnp.arange(x_prompt.shape[1], dtype=jnp.int32)
    y_prompt, new_p = trunk(x_prompt, pos_p, p, mem_prompt, None, None)
    pos_s = PAST_LEN + jnp.arange(x_sample.shape[1], dtype=jnp.int32)
    past = dict(mla_ckv=cache_mla_ckv, mla_kpe=cache_mla_kpe, swa_k=cache_swa_k, swa_v=cache_swa_v,
                mem_k=cache_mem_k, mem_v=cache_mem_v, conf_conv=state_conf_conv,
                sc_conv=state_sc_conv, ffn_conv=state_ffn_conv)
    y_sample, new_s = trunk(x_sample, pos_s, p, None, past, page_table)
    return (y_prompt, y_sample,
            new_p["mla_ckv"], new_p["mla_kpe"], new_p["swa_k"], new_p["swa_v"], new_p["mem_k"], new_p["mem_v"],
            new_p["conf_conv"], new_p["sc_conv"], new_p["ffn_conv"],
            new_s["mla_ckv"], new_s["mla_kpe"], new_s["swa_k"], new_s["swa_v"],
            new_s["conf_conv"], new_s["sc_conv"], new_s["ffn_conv"])
```

```python
import functools

import jax
import jax.numpy as jnp
from jax import lax
from jax.experimental import pallas as pl
from jax.experimental.pallas import tpu as pltpu

F32 = jnp.float32
BF16 = jnp.bfloat16

EPS = 1e-6
ROPE_THETA = 10000.0
WINDOW = 128
NEG_INF = -1e30

LANES = 128
SUBLANES = 8
VMEM_LIMIT_BYTES = 56 * 1024 * 1024


def _cparams(*sem):
    return pltpu.CompilerParams(dimension_semantics=sem, vmem_limit_bytes=VMEM_LIMIT_BYTES)


def _tile(n, pref, mult=SUBLANES):
    if n <= pref:
        return n
    t = (pref // mult) * mult
    while t > mult and n % t:
        t -= mult
    assert n % t == 0, (n, pref, mult)
    return t


def _rms(x, g):
    y = x * lax.rsqrt(jnp.mean(x * x, axis=-1, keepdims=True) + EPS)
    return y * g


def _dot(a, b):
    return jnp.dot(a, b, preferred_element_type=F32)


def _dot_nt(a, b):
    return lax.dot_general(a, b, (((1,), (1,)), ((), ())), preferred_element_type=F32)


def _mm_kernel(*refs, pre, post):
    it = iter(refs)
    x_ref, w_ref = next(it), next(it)
    gpre_ref = next(it) if pre else None
    gpost_ref, res_ref = (next(it), next(it)) if post else (None, None)
    o_ref = next(it)
    x = x_ref[...]
    if pre:
        x = _rms(x.astype(F32), gpre_ref[...])
    acc = _dot(x.astype(BF16), w_ref[...])
    if post:
        acc = res_ref[...] + _rms(acc, gpost_ref[...])
    o_ref[...] = acc.astype(o_ref.dtype)


def matmul_fused(x, w, *, g_pre=None, g_post=None, res=None, xcol=0, tm=256, tn=1024, out_dtype=F32):
    M = x.shape[0]
    K, N = w.shape
    pre, post = g_pre is not None, g_post is not None
    tm = _tile(M, tm)
    tn = N if post else _tile(N, tn, LANES)
    grid = (N // tn, M // tm)
    in_specs = [pl.BlockSpec((tm, K), lambda j, i: (i, xcol)),
                pl.BlockSpec((K, tn), lambda j, i: (0, j))]
    args = [x, w]
    if pre:
        in_specs.append(pl.BlockSpec((1, K), lambda j, i: (0, 0)))
        args.append(g_pre.reshape(1, K).astype(F32))
    if post:
        in_specs += [pl.BlockSpec((1, N), lambda j, i: (0, 0)),
                     pl.BlockSpec((tm, N), lambda j, i: (i, 0))]
        args += [g_post.reshape(1, N).astype(F32), res]
    return pl.pallas_call(
        functools.partial(_mm_kernel, pre=pre, post=post),
        grid=grid, in_specs=in_specs,
        out_specs=pl.BlockSpec((tm, tn), lambda j, i: (i, j)),
        out_shape=jax.ShapeDtypeStruct((M, N), out_dtype),
        compiler_params=_cparams("parallel", "parallel"),
    )(*args)


def _ffn_kernel(x_ref, gpre_ref, wg_ref, wu_ref, cwg_ref, cwu_ref, cbg_ref, cbu_ref, wd_ref,
                bufg_ref, bufu_ref, gpost_ref, y_ref, sg_ref, su_ref, xn_ref, carry_ref,
                *, seg, tiles_per_seq):
    i, j = pl.program_id(0), pl.program_id(1)
    nf = pl.num_programs(1)
    tm = x_ref.shape[0]
    tf = wg_ref.shape[1]

    @pl.when(j == 0)
    def _():
        xn_ref[...] = _rms(x_ref[...], gpre_ref[...]).astype(BF16)

    xn = xn_ref[...]
    halves = ((wg_ref, cwg_ref, cbg_ref, bufg_ref, sg_ref, 0), (wu_ref, cwu_ref, cbu_ref, bufu_ref, su_ref, 1))
    conv = []
    for w_ref, cw_ref, cb_ref, buf_ref, s_ref, hidx in halves:
        u = _dot(xn, w_ref[...])
        r1 = pltpu.roll(u, 1, axis=0)
        r2 = pltpu.roll(u, 2, axis=0)
        if seg is None:
            @pl.when((i % tiles_per_seq) == 0)
            def _(buf_ref=buf_ref, hidx=hidx):
                carry_ref[j, hidx] = buf_ref[0]

            prev = carry_ref[j, hidx]
            row = lax.broadcasted_iota(jnp.int32, (tm, 1), 0)
            s1 = jnp.where(row == 0, prev[1:2], r1)
            s2 = jnp.where(row == 0, prev[0:1], jnp.where(row == 1, prev[1:2], r2))
            carry_ref[j, hidx] = u[tm - 2:tm]
            s_ref[0] = u[tm - 2:tm]
            c = cw_ref[0:1] * s2 + cw_ref[1:2] * s1 + cw_ref[2:3] * u + cb_ref[...]
        else:
            nseq = tm // seg
            u3 = u.reshape(nseq, seg, tf)
            b0, b1 = buf_ref[:, 0:1, :], buf_ref[:, 1:2, :]
            t = lax.broadcasted_iota(jnp.int32, (1, seg, 1), 1)
            s1 = jnp.where(t == 0, b1, r1.reshape(nseq, seg, tf))
            s2 = jnp.where(t == 0, b0, jnp.where(t == 1, b1, r2.reshape(nseq, seg, tf)))
            s_ref[...] = u3[:, seg - 2:seg, :]
            c = (cw_ref[0:1][None] * s2 + cw_ref[1:2][None] * s1 + cw_ref[2:3][None] * u3
                 + cb_ref[...][None]).reshape(tm, tf)
        conv.append(c)
    gate, up = conv
    hmid = (gate * jax.nn.sigmoid(gate) * up).astype(BF16)
    contrib = _dot(hmid, wd_ref[...])

    @pl.when(j == 0)
    def _():
        y_ref[...] = contrib

    @pl.when(j > 0)
    def _():
        y_ref[...] += contrib

    @pl.when(j == nf - 1)
    def _():
        y_ref[...] = x_ref[...] + _rms(y_ref[...], gpost_ref[...])


def conv_ffn(x, g_pre, g_post, w_up, conv_w, conv_b, w_down, buf, *, seq_len, tm=512, tf=512):
    M, D = x.shape
    F = w_down.shape[0]
    nseq = M // seq_len
    tf = _tile(F, tf, LANES)
    nf = F // tf
    if seq_len >= LANES:
        tm = _tile(seq_len, tm)
        seg, tiles_per_seq, seq_per_tile = None, seq_len // tm, 1
        seq_map = lambda i: i // tiles_per_seq
    else:
        tm = _tile(M, tm, seq_len)
        seg, tiles_per_seq, seq_per_tile = seq_len, 1, tm // seq_len
        seq_map = lambda i: i
    row = lambda a: a.reshape(1, -1).astype(F32)
    y, sg, su = pl.pallas_call(
        functools.partial(_ffn_kernel, seg=seg, tiles_per_seq=tiles_per_seq),
        grid=(M // tm, nf),
        in_specs=[
            pl.BlockSpec((tm, D), lambda i, j: (i, 0)),
            pl.BlockSpec((1, D), lambda i, j: (0, 0)),
            pl.BlockSpec((D, tf), lambda i, j: (0, j)),
            pl.BlockSpec((D, tf), lambda i, j: (0, nf + j)),
            pl.BlockSpec((3, tf), lambda i, j: (0, j)),
            pl.BlockSpec((3, tf), lambda i, j: (0, nf + j)),
            pl.BlockSpec((1, tf), lambda i, j: (0, j)),
            pl.BlockSpec((1, tf), lambda i, j: (0, nf + j)),
            pl.BlockSpec((tf, D), lambda i, j: (j, 0)),
            pl.BlockSpec((seq_per_tile, 2, tf), lambda i, j: (seq_map(i), 0, j)),
            pl.BlockSpec((seq_per_tile, 2, tf), lambda i, j: (seq_map(i), 0, nf + j)),
            pl.BlockSpec((1, D), lambda i, j: (0, 0)),
        ],
        out_specs=[
            pl.BlockSpec((tm, D), lambda i, j: (i, 0)),
            pl.BlockSpec((seq_per_tile, 2, tf), lambda i, j: (seq_map(i), 0, j)),
            pl.BlockSpec((seq_per_tile, 2, tf), lambda i, j: (seq_map(i), 0, j)),
        ],
        out_shape=[jax.ShapeDtypeStruct((M, D), F32),
                   jax.ShapeDtypeStruct((nseq, 2, F), F32),
                   jax.ShapeDtypeStruct((nseq, 2, F), F32)],
        scratch_shapes=[pltpu.VMEM((tm, D), BF16), pltpu.VMEM((nf, 2, 2, tf), F32)],
        compiler_params=_cparams("arbitrary", "arbitrary"),
    )(x, row(g_pre), w_up, w_up, conv_w, conv_w, row(conv_b), row(conv_b), w_down, buf, buf, row(g_post))
    return y, jnp.concatenate([sg, su], axis=-1)


def _rope(x, cos, sin_signed):
    W = x.shape[-1]
    lane = lax.broadcasted_iota(jnp.int32, (1, W), 1)
    swapped = jnp.where((lane & 63) < 32, pltpu.roll(x, W - 32, axis=1), pltpu.roll(x, 32, axis=1))
    return x * cos + swapped * sin_signed


def rope_tables(pos, width, rows):
    inv = ROPE_THETA ** (-jnp.arange(0, 64, 2, dtype=F32) / 64)
    ang = pos.astype(F32)[:, None] * inv[None, :]
    c, s = jnp.cos(ang), jnp.sin(ang)
    cos = jnp.tile(jnp.concatenate([c, c], axis=-1), (rows // pos.shape[0], width // 64))
    sin = jnp.tile(jnp.concatenate([-s, s], axis=-1), (rows // pos.shape[0], width // 64))
    return cos, sin


def _mla_pre_kernel(q_ref, ckv_ref, kpe_ref, gkv_ref, wuk_ref, cos_ref, sin_ref,
                    qcat_ref, kcat_ref, ckvn_ref, kper_ref, *, heads, nope, lora):
    cos, sin = cos_ref[...], sin_ref[...]
    q = q_ref[...]
    qpe = _rope(q[:, heads * nope:], cos, sin)
    for h in range(heads):
        qn = q[:, h * nope:(h + 1) * nope].astype(BF16)
        qcat_ref[h, :, 0:lora] = _dot(qn, wuk_ref[h]).astype(BF16)
        qcat_ref[h, :, lora:lora + LANES] = qpe[:, h * LANES:(h + 1) * LANES].astype(BF16)
    ckvn = _rms(ckv_ref[...], gkv_ref[...])
    kper = _rope(kpe_ref[...], cos[:, :LANES], sin[:, :LANES])
    ckvn_ref[...] = ckvn
    kper_ref[...] = kper
    kcat_ref[:, 0:lora] = ckvn.astype(BF16)
    kcat_ref[:, lora:lora + LANES] = kper.astype(BF16)


def mla_pre(q, z, g_kv, w_ukT, cos, sin, *, ckv_col, kpe_col, tm=256):
    M = q.shape[0]
    heads, nope, lora = w_ukT.shape
    tm = _tile(M, min(tm, cos.shape[0]))
    ntab = cos.shape[0] // tm
    wq = q.shape[1]
    kw = lora + LANES
    return pl.pallas_call(
        functools.partial(_mla_pre_kernel, heads=heads, nope=nope, lora=lora),
        grid=(M // tm,),
        in_specs=[
            pl.BlockSpec((tm, wq), lambda i: (i, 0)),
            pl.BlockSpec((tm, lora), lambda i: (i, ckv_col)),
            pl.BlockSpec((tm, LANES), lambda i: (i, kpe_col)),
            pl.BlockSpec((1, lora), lambda i: (0, 0)),
            pl.BlockSpec((heads, nope, lora), lambda i: (0, 0, 0)),
            pl.BlockSpec((tm, heads * LANES), lambda i: (i % ntab, 0)),
            pl.BlockSpec((tm, heads * LANES), lambda i: (i % ntab, 0)),
        ],
        out_specs=[
            pl.BlockSpec((heads, tm, kw), lambda i: (0, i, 0)),
            pl.BlockSpec((tm, kw), lambda i: (i, 0)),
            pl.BlockSpec((tm, lora), lambda i: (i, 0)),
            pl.BlockSpec((tm, LANES), lambda i: (i, 0)),
        ],
        out_shape=[jax.ShapeDtypeStruct((heads, M, kw), BF16),
                   jax.ShapeDtypeStruct((M, kw), BF16),
                   jax.ShapeDtypeStruct((M, lora), F32),
                   jax.ShapeDtypeStruct((M, LANES), F32)],
        compiler_params=_cparams("parallel"),
    )(q, z, z, g_kv.reshape(1, lora).astype(F32), w_ukT, cos, sin)


def _mla_prompt_kernel(q_ref, k_ref, wuv_ref, y_ref, m_ref, l_ref, acc_ref, *, scale, lora, vdim):
    qi, ki = pl.program_id(1), pl.program_id(2)
    heads, tq, kw = q_ref.shape
    tk = k_ref.shape[0]

    @pl.when(ki == 0)
    def _():
        m_ref[...] = jnp.full_like(m_ref, NEG_INF)
        l_ref[...] = jnp.zeros_like(l_ref)
        acc_ref[...] = jnp.zeros_like(acc_ref)

    @pl.when(ki * tk <= qi * tq + tq - 1)
    def _():
        q = q_ref[...].reshape(heads * tq, kw)
        k = k_ref[...]
        s = _dot_nt(q, k) * scale
        qpos = qi * tq + (lax.broadcasted_iota(jnp.int32, (heads * tq, 1), 0) & (tq - 1))
        kpos = ki * tk + lax.broadcasted_iota(jnp.int32, (1, tk), 1)
        s = jnp.where(kpos <= qpos, s, NEG_INF)
        m_old = m_ref[...]
        m_new = jnp.maximum(m_old, jnp.max(s, axis=-1, keepdims=True))
        alpha = jnp.exp(m_old - m_new)
        p = jnp.exp(s - m_new)
        l_ref[...] = alpha * l_ref[...] + jnp.sum(p, axis=-1, keepdims=True)
        acc_ref[...] = alpha * acc_ref[...] + _dot(p.astype(BF16), k[:, :lora])
        m_ref[...] = m_new

    @pl.when(ki == pl.num_programs(2) - 1)
    def _():
        o = (acc_ref[...] / l_ref[...]).astype(BF16)
        for h in range(heads):
            y_ref[:, h * vdim:(h + 1) * vdim] = _dot(o[h * tq:(h + 1) * tq], wuv_ref[:, h * vdim:(h + 1) * vdim])


def mla_attn_prompt(qcat, kcat, w_uv, *, batch, scale, tq=128, tk=256):
    heads, M, kw = qcat.shape
    S = M // batch
    lora, hv = w_uv.shape
    vdim = hv // heads
    tq = _tile(S, tq)
    tk = _tile(S, tk)
    assert tq & (tq - 1) == 0
    nq, nk = S // tq, S // tk

    def kmap(b, qi, ki):
        return (b * nk + jnp.minimum(ki, (qi * tq + tq - 1) // tk), 0)

    return pl.pallas_call(
        functools.partial(_mla_prompt_kernel, scale=scale, lora=lora, vdim=vdim),
        grid=(batch, nq, nk),
        in_specs=[
            pl.BlockSpec((heads, tq, kw), lambda b, qi, ki: (0, b * nq + qi, 0)),
            pl.BlockSpec((tk, kw), kmap),
            pl.BlockSpec((lora, hv), lambda b, qi, ki: (0, 0)),
        ],
        out_specs=pl.BlockSpec((tq, hv), lambda b, qi, ki: (b * nq + qi, 0)),
        out_shape=jax.ShapeDtypeStruct((M, hv), F32),
        scratch_shapes=[pltpu.VMEM((heads * tq, 1), F32), pltpu.VMEM((heads * tq, 1), F32),
                        pltpu.VMEM((heads * tq, lora), F32)],
        compiler_params=_cparams("parallel", "parallel", "arbitrary"),
    )(qcat, kcat, w_uv)


def _mla_sample_kernel(pt_ref, q_ref, *refs, scale, lora, rope, vdim, pages, seq):
    ckv_refs, kpe_refs = refs[:pages], refs[pages:2 * pages]
    ckvn_ref, kper_ref, wuv_ref, y_ref, m_ref, l_ref, acc_ref, knew_ref = refs[2 * pages:]
    step = pl.program_id(1)
    rows = q_ref.shape[1]
    page = ckv_refs[0].shape[0]

    @pl.when(step == 0)
    def _():
        m_ref[...] = jnp.full_like(m_ref, NEG_INF)
        l_ref[...] = jnp.zeros_like(l_ref)
        acc_ref[...] = jnp.zeros_like(acc_ref)

    q = q_ref[0]
    q_lat, q_pe = q[:, :lora], q[:, lora:lora + rope]

    def update(s, vals):
        m_old = m_ref[...]
        m_new = jnp.maximum(m_old, jnp.max(s, axis=-1, keepdims=True))
        alpha = jnp.exp(m_old - m_new)
        p = jnp.exp(s - m_new)
        l_ref[...] = alpha * l_ref[...] + jnp.sum(p, axis=-1, keepdims=True)
        pb = p.astype(BF16)
        acc = alpha * acc_ref[...]
        for r, v in enumerate(vals):
            acc = acc + _dot(pb[:, r * page:(r + 1) * page], v)
        acc_ref[...] = acc
        m_ref[...] = m_new

    vals, scores = [], []
    for r in range(pages):
        kc = ckv_refs[r][...].astype(BF16)
        kp = kpe_refs[r][...].astype(BF16)
        scores.append((_dot_nt(q_lat, kc) + _dot_nt(q_pe, kp)) * scale)
        vals.append(kc)
    update(jnp.concatenate(scores, axis=-1), vals)

    @pl.when(step == pl.num_programs(1) - 1)
    def _():
        knew_ref[...] = jnp.zeros_like(knew_ref)
        knew_ref[0:seq, 0:lora] = ckvn_ref[...]
        knew_ref[0:seq, lora:lora + LANES] = kper_ref[...]
        kn = knew_ref[...].astype(BF16)
        s = _dot_nt(q, kn) * scale
        qidx = lax.broadcasted_iota(jnp.int32, (rows, 1), 0) & (seq - 1)
        kidx = lax.broadcasted_iota(jnp.int32, (1, page), 1)
        update(jnp.where(kidx <= qidx, s, NEG_INF), [kn[:, :lora]])
        o = (acc_ref[...] / l_ref[...]).astype(BF16)
        full = _dot(o, wuv_ref[...])
        heads = rows // seq
        for h in range(heads):
            y_ref[:, h * vdim:(h + 1) * vdim] = full[h * seq:(h + 1) * seq, h * vdim:(h + 1) * vdim]


def mla_attn_sample(qs, pool_ckv, pool_kpe, page_table, ckv_n, kpe_r, w_uv, *, scale, pages=16):
    B, rows, kw = qs.shape
    _, page, lora = pool_ckv.shape
    rope = pool_kpe.shape[-1]
    n_pages = page_table.shape[1]
    pages = _tile(n_pages, pages, 1)
    hv = w_uv.shape[1]
    seq = ckv_n.shape[0] // B
    heads = rows // seq
    vdim = hv // heads
    assert seq & (seq - 1) == 0 and seq <= page

    def pmap(r):
        return lambda b, s, pt: (pt[b, s * pages + r], 0, 0)

    in_specs = [pl.BlockSpec((1, rows, kw), lambda b, s, pt: (b, 0, 0))]
    in_specs += [pl.BlockSpec((None, page, lora), pmap(r)) for r in range(pages)]
    in_specs += [pl.BlockSpec((None, page, rope), pmap(r)) for r in range(pages)]
    in_specs += [pl.BlockSpec((seq, lora), lambda b, s, pt: (b, 0)),
                 pl.BlockSpec((seq, LANES), lambda b, s, pt: (b, 0)),
                 pl.BlockSpec((lora, hv), lambda b, s, pt: (0, 0))]
    return pl.pallas_call(
        functools.partial(_mla_sample_kernel, scale=scale, lora=lora, rope=rope, vdim=vdim, pages=pages, seq=seq),
        grid_spec=pltpu.PrefetchScalarGridSpec(
            num_scalar_prefetch=1, grid=(B, n_pages // pages), in_specs=in_specs,
            out_specs=pl.BlockSpec((seq, hv), lambda b, s, pt: (b, 0)),
            scratch_shapes=[pltpu.VMEM((rows, 1), F32), pltpu.VMEM((rows, 1), F32),
                            pltpu.VMEM((rows, lora), F32), pltpu.VMEM((page, kw), F32)]),
        out_shape=jax.ShapeDtypeStruct((B * seq, hv), F32),
        compiler_params=_cparams("parallel", "arbitrary"),
    )(page_table, qs, *([pool_ckv] * pages), *([pool_kpe] * pages), ckv_n, kpe_r, w_uv)


def _dwconv_kernel(*refs, mode, taps, hist):
    if mode == "conf":
        a_ref, b_ref, st_ref, w_ref, cb_ref, lg_ref, lb_ref, y_ref, so_ref, buf_ref, c_ref = refs
    else:
        gate_ref, a_ref, b_ref, st_ref, w_ref, y_ref, so_ref, buf_ref, c_ref = refs
    t = pl.program_id(1)
    tt, C = a_ref.shape

    @pl.when(t == 0)
    def _():
        buf_ref[0:hist] = st_ref[0]

    if mode == "conf":
        u = a_ref[...] * jax.nn.sigmoid(b_ref[...])
    else:
        u = a_ref[...] * b_ref[...]
    buf_ref[hist:hist + tt] = u
    rc = min(tt, 64)
    for c0 in range(0, C, LANES):
        for r0 in range(0, tt, rc):
            acc = None
            for k in range(taps):
                term = (w_ref[k:k + 1, c0:c0 + LANES]
                        * buf_ref[r0 + hist - (taps - 1) + k:r0 + hist - (taps - 1) + k + rc, c0:c0 + LANES])
                acc = term if acc is None else acc + term
            c_ref[r0:r0 + rc, c0:c0 + LANES] = acc
    c = c_ref[...]
    if mode == "conf":
        c = c + cb_ref[...]
        mu = jnp.mean(c, axis=-1, keepdims=True)
        d = c - mu
        var = jnp.mean(d * d, axis=-1, keepdims=True)
        yn = d * lax.rsqrt(var + EPS) * lg_ref[...] + lb_ref[...]
        y_ref[...] = yn * jax.nn.sigmoid(yn)
    else:
        y_ref[...] = gate_ref[...] * c
    so_ref[0] = buf_ref[tt:tt + hist]
    if tt >= hist:
        buf_ref[0:hist] = buf_ref[tt:tt + hist]


def dwconv(z, state, w, *, mode, cols, batch, extra=(), tt=128):
    M = z.shape[0]
    taps, C = w.shape
    S = M // batch
    hist = -(-(taps - 1) // SUBLANES) * SUBLANES
    tt = _tile(S, tt)
    nt = S // tt
    assert nt == 1 or tt >= hist
    st = jnp.pad(state.astype(F32), ((0, 0), (hist - (taps - 1), 0), (0, 0)))
    colspec = lambda c: pl.BlockSpec((tt, C), lambda b, t: (b * nt + t, c))
    vec = pl.BlockSpec((1, C), lambda b, t: (0, 0))
    in_specs = [colspec(c) for c in cols]
    in_specs += [pl.BlockSpec((1, hist, C), lambda b, t: (b, 0, 0)), pl.BlockSpec((taps, C), lambda b, t: (0, 0))]
    in_specs += [vec] * len(extra)
    y, so = pl.pallas_call(
        functools.partial(_dwconv_kernel, mode=mode, taps=taps, hist=hist),
        grid=(batch, nt),
        in_specs=in_specs,
        out_specs=[pl.BlockSpec((tt, C), lambda b, t: (b * nt + t, 0)),
                   pl.BlockSpec((1, hist, C), lambda b, t: (b, 0, 0))],
        out_shape=[jax.ShapeDtypeStruct((M, C), F32), jax.ShapeDtypeStruct((batch, hist, C), F32)],
        scratch_shapes=[pltpu.VMEM((hist + tt, C), F32), pltpu.VMEM((tt, C), F32)],
        compiler_params=_cparams("parallel", "arbitrary"),
    )(*([z] * len(cols)), st, w.astype(F32), *[e.reshape(1, C).astype(F32) for e in extra])
    return y, so[:, hist - (taps - 1):]


def _swa_rope_kernel(q_ref, k_ref, cos_ref, sin_ref, qo_ref, ko_ref):
    cos, sin = cos_ref[...], sin_ref[...]
    kw = k_ref.shape[1]
    qo_ref[...] = _rope(q_ref[...], cos, sin)
    ko_ref[...] = _rope(k_ref[...], cos[:, :kw], sin[:, :kw])


def swa_rope(z, cos, sin, *, q_col, k_col, qw, kw, tm=256):
    M = z.shape[0]
    tm = _tile(M, min(tm, cos.shape[0]))
    ntab = cos.shape[0] // tm
    return pl.pallas_call(
        _swa_rope_kernel,
        grid=(M // tm,),
        in_specs=[pl.BlockSpec((tm, qw), lambda i: (i, q_col)),
                  pl.BlockSpec((tm, kw), lambda i: (i, k_col)),
                  pl.BlockSpec((tm, qw), lambda i: (i % ntab, 0)),
                  pl.BlockSpec((tm, qw), lambda i: (i % ntab, 0))],
        out_specs=[pl.BlockSpec((tm, qw), lambda i: (i, 0)), pl.BlockSpec((tm, kw), lambda i: (i, 0))],
        out_shape=[jax.ShapeDtypeStruct((M, qw), F32), jax.ShapeDtypeStruct((M, kw), F32)],
        compiler_params=_cparams("parallel"),
    )(z, z, cos, sin)


def _sink_softmax_pv(s, sink, v):
    m = jnp.maximum(jnp.max(s, axis=-1, keepdims=True), sink)
    p = jnp.exp(s - m)
    denom = jnp.sum(p, axis=-1, keepdims=True) + jnp.exp(sink - m)
    return _dot(p.astype(BF16), v) / denom


def _swa_prompt_kernel(sink_ref, q_ref, kc_ref, kp_ref, vc_ref, vp_ref, y_ref, *, scale, hd, group):
    n = pl.program_id(1)
    W = q_ref.shape[0]
    kvh = kc_ref.shape[1] // hd
    qi = lax.broadcasted_iota(jnp.int32, (W, 1), 0)
    si = lax.broadcasted_iota(jnp.int32, (1, 2 * W), 1)
    rel = qi + W - si
    valid = (rel >= 0) & (rel < W) & (n * W + si - W >= 0)
    q = q_ref[...].astype(BF16)
    kk = jnp.concatenate([kp_ref[...], kc_ref[...]], axis=0).astype(BF16)
    vv = jnp.concatenate([vp_ref[...], vc_ref[...]], axis=0).astype(BF16)
    outs = []
    for kv in range(kvh):
        k_h = kk[:, kv * hd:(kv + 1) * hd]
        v_h = vv[:, kv * hd:(kv + 1) * hd]
        for g in range(group):
            h = kv * group + g
            s = _dot_nt(q[:, h * hd:(h + 1) * hd], k_h) * scale
            s = jnp.where(valid, s, NEG_INF)
            outs.append(_sink_softmax_pv(s, sink_ref[h], v_h))
    y_ref[...] = jnp.concatenate(outs, axis=-1)


def swa_prompt(q, k, z, sinks, *, v_col, batch, scale, hd):
    M, qw = q.shape
    kw = k.shape[1]
    S = M // batch
    W = WINDOW
    nb = S // W
    group = qw // kw
    cur = lambda b, n: b * nb + n
    prev = lambda b, n: b * nb + jnp.maximum(n - 1, 0)
    return pl.pallas_call(
        functools.partial(_swa_prompt_kernel, scale=scale, hd=hd, group=group),
        grid=(batch, nb),
        in_specs=[pl.BlockSpec(memory_space=pltpu.SMEM),
                  pl.BlockSpec((W, qw), lambda b, n: (cur(b, n), 0)),
                  pl.BlockSpec((W, kw), lambda b, n: (cur(b, n), 0)),
                  pl.BlockSpec((W, kw), lambda b, n: (prev(b, n), 0)),
                  pl.BlockSpec((W, kw), lambda b, n: (cur(b, n), v_col)),
                  pl.BlockSpec((W, kw), lambda b, n: (prev(b, n), v_col))],
        out_specs=pl.BlockSpec((W, qw), lambda b, n: (cur(b, n), 0)),
        out_shape=jax.ShapeDtypeStruct((M, qw), F32),
        compiler_params=_cparams("parallel", "parallel"),
    )(sinks.astype(F32), q, k, k, z, z)


def _swa_sample_kernel(sink_ref, q_ref, kb_ref, vb_ref, kn_ref, vn_ref, y_ref, ko_ref, vo_ref,
                       ks_ref, vs_ref, *, scale, hd, group, first_pos):
    seq = q_ref.shape[0]
    nbuf = kb_ref.shape[1]
    npad = ks_ref.shape[0]
    kvh = kn_ref.shape[1] // hd
    for s_ref, b_ref, n_ref, o_ref in ((ks_ref, kb_ref, kn_ref, ko_ref), (vs_ref, vb_ref, vn_ref, vo_ref)):
        s_ref[0:nbuf] = b_ref[0]
        s_ref[nbuf:nbuf + seq] = n_ref[...]
        s_ref[nbuf + seq:npad] = jnp.zeros((npad - nbuf - seq, s_ref.shape[1]), F32)
        o_ref[0] = s_ref[seq:seq + nbuf]
    kk = ks_ref[...].astype(BF16)
    vv = vs_ref[...].astype(BF16)
    qi = lax.broadcasted_iota(jnp.int32, (group * seq, 1), 0) & (seq - 1)
    ji = lax.broadcasted_iota(jnp.int32, (1, npad), 1)
    rel = qi + nbuf - ji
    valid = (rel >= 0) & (rel < WINDOW) & (first_pos + ji >= 0)
    q = q_ref[...]
    outs = []
    for kv in range(kvh):
        q4 = jnp.concatenate([q[:, (kv * group + g) * hd:(kv * group + g + 1) * hd] for g in range(group)], axis=0)
        sink = jnp.concatenate([jnp.full((seq, 1), sink_ref[kv * group + g], F32) for g in range(group)], axis=0)
        s = _dot_nt(q4.astype(BF16), kk[:, kv * hd:(kv + 1) * hd]) * scale
        s = jnp.where(valid, s, NEG_INF)
        o = _sink_softmax_pv(s, sink, vv[:, kv * hd:(kv + 1) * hd])
        outs += [o[g * seq:(g + 1) * seq] for g in range(group)]
    y_ref[...] = jnp.concatenate(outs, axis=-1)


def swa_sample(q, k_new, z, buf_k, buf_v, sinks, *, v_col, scale, hd, past_len):
    M, qw = q.shape
    B, nbuf, kw = buf_k.shape
    seq = M // B
    group = qw // kw
    npad = 2 * nbuf
    assert seq & (seq - 1) == 0 and nbuf + seq <= npad and nbuf == WINDOW
    row = lambda b: (b, 0)
    return pl.pallas_call(
        functools.partial(_swa_sample_kernel, scale=scale, hd=hd, group=group, first_pos=past_len - nbuf),
        grid=(B,),
        in_specs=[pl.BlockSpec(memory_space=pltpu.SMEM),
                  pl.BlockSpec((seq, qw), row),
                  pl.BlockSpec((1, nbuf, kw), lambda b: (b, 0, 0)),
                  pl.BlockSpec((1, nbuf, kw), lambda b: (b, 0, 0)),
                  pl.BlockSpec((seq, kw), row),
                  pl.BlockSpec((seq, kw), lambda b: (b, v_col))],
        out_specs=[pl.BlockSpec((seq, qw), row),
                   pl.BlockSpec((1, nbuf, kw), lambda b: (b, 0, 0)),
                   pl.BlockSpec((1, nbuf, kw), lambda b: (b, 0, 0))],
        out_shape=[jax.ShapeDtypeStruct((M, qw), F32),
                   jax.ShapeDtypeStruct((B, nbuf, kw), F32),
                   jax.ShapeDtypeStruct((B, nbuf, kw), F32)],
        scratch_shapes=[pltpu.VMEM((npad, kw), F32), pltpu.VMEM((npad, kw), F32)],
        compiler_params=_cparams("parallel"),
    )(sinks.astype(F32), q, buf_k, buf_v, k_new, z)


def _mem_attn_kernel(q_ref, k_ref, v_ref, o_ref, *, scale, hd, mxu_dtype):
    heads = q_ref.shape[1] // hd
    q = q_ref[...].astype(mxu_dtype)
    k = k_ref[...].astype(mxu_dtype)
    v = v_ref[...].astype(mxu_dtype)
    outs = []
    for h in range(heads):
        sl = slice(h * hd, (h + 1) * hd)
        s = _dot_nt(q[:, sl], k[:, sl]) * scale
        m = jnp.max(s, axis=-1, keepdims=True)
        p = jnp.exp(s - m)
        p = p / jnp.sum(p, axis=-1, keepdims=True)
        outs.append(_dot(p.astype(mxu_dtype), v[:, sl]))
    o_ref[...] = jnp.concatenate(outs, axis=-1)


def mem_attn(q, k_arr, v_arr, *, k_col, v_col, batch, mem_tokens, scale, hd, tq=256):
    M, w = q.shape
    S = M // batch
    tq = _tile(S, tq)
    nq = S // tq
    mxu_dtype = BF16 if tq >= 16 else F32
    return pl.pallas_call(
        functools.partial(_mem_attn_kernel, scale=scale, hd=hd, mxu_dtype=mxu_dtype),
        grid=(batch, nq),
        in_specs=[pl.BlockSpec((tq, w), lambda b, i: (b * nq + i, 0)),
                  pl.BlockSpec((mem_tokens, w), lambda b, i: (b, k_col)),
                  pl.BlockSpec((mem_tokens, w), lambda b, i: (b, v_col))],
        out_specs=pl.BlockSpec((tq, w), lambda b, i: (b * nq + i, 0)),
        out_shape=jax.ShapeDtypeStruct((M, w), F32),
        compiler_params=_cparams("parallel", "parallel"),
    )(q, k_arr, v_arr)


def _prep_weights(p):
    dims = p["dims"]
    H, nope, rope, lora, qlora = dims["mla_heads"], dims["nope"], dims["rope"], dims["lora"], dims["qlora"]
    w = {}
    wi = p["w_even_in"]
    n_in = wi.shape[-1]
    pad = jnp.zeros(wi.shape[:2] + (-(-(n_in + LANES - rope) // 512) * 512 - n_in,), wi.dtype)
    w["even_in"] = jnp.concatenate(
        [wi[..., qlora + lora + rope:], wi[..., :qlora + lora + rope], pad], axis=-1).astype(BF16)
    wq = p["w_q_b"].reshape(p["w_q_b"].shape[0], qlora, H, nope + rope)
    wq_pe = jnp.pad(wq[..., nope:], ((0, 0), (0, 0), (0, 0), (0, LANES - rope)))
    w["q_b"] = jnp.concatenate([wq[..., :nope].reshape(-1, qlora, H * nope),
                                wq_pe.reshape(-1, qlora, H * LANES)], axis=-1).astype(BF16)
    w["ukT"] = jnp.transpose(p["w_uk"], (0, 2, 3, 1)).astype(BF16)
    w["uv"] = p["w_uv"].reshape(p["w_uv"].shape[0], lora, -1).astype(BF16)
    for name in ("w_even_out", "w_odd_in", "w_odd_out", "w_mem_q", "w_mem_kv", "w_mem_o", "w_ffn_up", "w_ffn_down"):
        w[name] = p[name].astype(BF16)
    return w


def _trunk(x, pos, p, w, mem, past, page_table):
    dims = p["dims"]
    B, S, D = x.shape
    M = B * S
    H, nope, rope, lora, qlora = dims["mla_heads"], dims["nope"], dims["rope"], dims["lora"], dims["qlora"]
    conf_ch, sc_ch = dims["conf_ch"], dims["sc_ch"]
    swa_h, swa_kv, swa_hd = dims["swa_heads"], dims["swa_kv"], dims["swa_hd"]
    mem_h, mem_hd = dims["mem_heads"], dims["mem_hd"]
    mem_w = mem_h * mem_hd
    ff = dims["d_ff"]
    depth = p["norm_gains"].shape[0]
    prompt = past is None
    x = x.reshape(M, D)
    tab_rows = S if prompt else max(S, min(M, 256))
    cos, sin = rope_tables(pos, max(H * LANES, swa_h * swa_hd), tab_rows)
    new = {k: [] for k in ("mla_ckv", "mla_kpe", "swa_k", "swa_v", "mem_k", "mem_v", "conf_conv", "sc_conv", "ffn_conv")}
    for layer in range(depth):
        g = p["norm_gains"][layer]
        if layer % 2 == 0:
            i = layer // 2
            z = matmul_fused(x, w["even_in"][i], g_pre=g[0])
            glu_w = 2 * conf_ch
            q = matmul_fused(z, w["q_b"][i], g_pre=p["g_q_a"][i], xcol=glu_w // qlora)
            qcat, kcat, ckv_n, kpe_r = mla_pre(
                q, z, p["g_kv_a"][i], w["ukT"][i], cos[:, :H * LANES], sin[:, :H * LANES],
                ckv_col=(glu_w + qlora) // lora, kpe_col=(glu_w + qlora + lora) // LANES)
            scale = (nope + rope) ** -0.5
            if prompt:
                y_mla = mla_attn_prompt(qcat, kcat, w["uv"][i], batch=B, scale=scale)
                conf_state = jnp.zeros((B, p["conf_dw_w"].shape[1] - 1, conf_ch), F32)
            else:
                qs = qcat.reshape(H, B, S, lora + LANES).transpose(1, 0, 2, 3).reshape(B, H * S, lora + LANES)
                y_mla = mla_attn_sample(qs, past["mla_ckv"][i], past["mla_kpe"][i], page_table,
                                        ckv_n, kpe_r, w["uv"][i], scale=scale)
                conf_state = past["conf_conv"][i]
            c, conf_new = dwconv(z, conf_state, p["conf_dw_w"][i], mode="conf", cols=(0, 1), batch=B,
                                 extra=(p["conf_dw_b"][i], p["conf_ln_g"][i], p["conf_ln_b"][i]))
            mix = matmul_fused(jnp.concatenate([y_mla, c], axis=-1), w["w_even_out"][i], g_post=g[1], res=x)
            new["mla_ckv"].append(ckv_n.reshape(B, S, lora))
            new["mla_kpe"].append(kpe_r[:, :rope].reshape(B, S, rope))
            new["conf_conv"].append(conf_new)
        else:
            j = layer // 2
            z = matmul_fused(x, w["w_odd_in"][j], g_pre=g[0])
            qw, kw = swa_h * swa_hd, swa_kv * swa_hd
            sc_state = jnp.zeros((B, p["sc_conv_w"].shape[1] - 1, sc_ch), F32) if prompt else past["sc_conv"][j]
            y_sc, sc_new = dwconv(z, sc_state, p["sc_conv_w"][j], mode="sc", cols=(0, 1, 2), batch=B)
            q_r, k_r = swa_rope(z, cos[:, :qw], sin[:, :qw], q_col=3 * sc_ch // qw, k_col=(3 * sc_ch + qw) // kw,
                                qw=qw, kw=kw)
            v_col = (3 * sc_ch + qw + kw) // kw
            scale = swa_hd ** -0.5
            if prompt:
                y_att = swa_prompt(q_r, k_r, z, p["swa_sinks"][j], v_col=v_col, batch=B, scale=scale, hd=swa_hd)
                v = z[:, v_col * kw:(v_col + 1) * kw]
                nbuf = min(WINDOW, S)
                nk = k_r.reshape(B, S, swa_kv, swa_hd)[:, S - nbuf:]
                nv = v.reshape(B, S, swa_kv, swa_hd)[:, S - nbuf:]
            else:
                nbuf = past["swa_k"].shape[2]
                y_att, nk, nv = swa_sample(q_r, k_r, z, past["swa_k"][j].reshape(B, nbuf, kw),
                                           past["swa_v"][j].reshape(B, nbuf, kw), p["swa_sinks"][j],
                                           v_col=v_col, scale=scale, hd=swa_hd, past_len=past["past_len"])
                nk = nk.reshape(B, nbuf, swa_kv, swa_hd)
                nv = nv.reshape(B, nbuf, swa_kv, swa_hd)
            mix = matmul_fused(jnp.concatenate([y_sc, y_att], axis=-1), w["w_odd_out"][j], g_post=g[1], res=x)
            new["swa_k"].append(nk)
            new["swa_v"].append(nv)
            new["sc_conv"].append(sc_new)
        x = mix
        mem_tokens = mem.shape[1] if prompt else past["mem_k"].shape[2]
        qm = matmul_fused(x, w["w_mem_q"][layer], g_pre=g[2])
        if prompt:
            kv = matmul_fused(mem.reshape(-1, D), w["w_mem_kv"][layer])
            o = mem_attn(qm, kv, kv, k_col=0, v_col=1, batch=B, mem_tokens=mem_tokens, scale=mem_hd ** -0.5, hd=mem_hd)
            kv = kv.reshape(B, mem_tokens, 2, mem_h, mem_hd)
            new["mem_k"].append(kv[:, :, 0])
            new["mem_v"].append(kv[:, :, 1])
        else:
            o = mem_attn(qm, past["mem_k"][layer].reshape(-1, mem_w), past["mem_v"][layer].reshape(-1, mem_w),
                         k_col=0, v_col=0, batch=B, mem_tokens=mem_tokens, scale=mem_hd ** -0.5, hd=mem_hd)
        x = matmul_fused(o, w["w_mem_o"][layer], g_post=g[3], res=x)
        ffn_buf = jnp.zeros((B, 2, 2 * ff), F32) if prompt else past["ffn_conv"][layer]
        x, ffn_new = conv_ffn(x, g[4], g[5], w["w_ffn_up"][layer], p["ffn_conv_w"][layer], p["ffn_conv_b"][layer],
                              w["w_ffn_down"][layer], ffn_buf, seq_len=S)
        new["ffn_conv"].append(ffn_new)
    return x.reshape(B, S, D), {k: jnp.stack(v) for k, v in new.items() if v}


def kernel(x_prompt, x_sample, cache_mla_ckv, cache_mla_kpe, cache_swa_k, cache_swa_v, cache_mem_k, cache_mem_v,
           state_conf_conv, state_sc_conv, state_ffn_conv, page_table, mem_prompt, norm_gains, w_even_in, g_q_a,
           w_q_b, g_kv_a, w_uk, w_uv, conf_dw_w, conf_dw_b, conf_ln_g, conf_ln_b, w_even_out, w_odd_in, sc_conv_w,
           swa_sinks, w_odd_out, w_mem_q, w_mem_kv, w_mem_o, w_ffn_up, ffn_conv_w, ffn_conv_b, w_ffn_down):
    lora, heads, nope = w_uk.shape[1], w_uk.shape[2], w_uk.shape[3]
    rope = cache_mla_kpe.shape[-1]
    dims = dict(mla_heads=heads, nope=nope, rope=rope, lora=lora, qlora=g_q_a.shape[1],
                conf_ch=conf_dw_w.shape[2], sc_ch=sc_conv_w.shape[2],
                swa_heads=swa_sinks.shape[1], swa_kv=cache_swa_k.shape[3], swa_hd=cache_swa_k.shape[4],
                mem_heads=cache_mem_k.shape[3], mem_hd=cache_mem_k.shape[4], d_ff=w_ffn_down.shape[1])
    assert rope == 64 and dims["swa_hd"] == 64, "rotary helper assumes 64-wide rotary heads"
    p = dict(dims=dims, norm_gains=norm_gains, w_even_in=w_even_in, g_q_a=g_q_a, w_q_b=w_q_b, g_kv_a=g_kv_a,
             w_uk=w_uk, w_uv=w_uv, conf_dw_w=conf_dw_w, conf_dw_b=conf_dw_b, conf_ln_g=conf_ln_g,
             conf_ln_b=conf_ln_b, w_even_out=w_even_out, w_odd_in=w_odd_in, sc_conv_w=sc_conv_w,
             swa_sinks=swa_sinks, w_odd_out=w_odd_out, w_mem_q=w_mem_q, w_mem_kv=w_mem_kv, w_mem_o=w_mem_o,
             w_ffn_up=w_ffn_up, ffn_conv_w=ffn_conv_w, ffn_conv_b=ffn_conv_b, w_ffn_down=w_ffn_down)
    w = _prep_weights(p)
    pos_p = jnp.arange(x_prompt.shape[1], dtype=jnp.int32)
    y_prompt, new_p = _trunk(x_prompt, pos_p, p, w, mem_prompt, None, None)
    past_len = page_table.shape[1] * cache_mla_ckv.shape[2]
    pos_s = past_len + jnp.arange(x_sample.shape[1], dtype=jnp.int32)
    past = dict(mla_ckv=cache_mla_ckv, mla_kpe=cache_mla_kpe, swa_k=cache_swa_k, swa_v=cache_swa_v,
                mem_k=cache_mem_k, mem_v=cache_mem_v, conf_conv=state_conf_conv, sc_conv=state_sc_conv,
                ffn_conv=state_ffn_conv, past_len=past_len)
    y_sample, new_s = _trunk(x_sample, pos_s, p, w, None, past, page_table)
    return (y_prompt, y_sample,
            new_p["mla_ckv"], new_p["mla_kpe"], new_p["swa_k"], new_p["swa_v"], new_p["mem_k"], new_p["mem_v"],
            new_p["conf_conv"], new_p["sc_conv"], new_p["ffn_conv"],
            new_s["mla_ckv"], new_s["mla_kpe"], new_s["swa_k"], new_s["swa_v"],
            new_s["conf_conv"], new_s["sc_conv"], new_s["ffn_conv"])
```

```python
import functools

import jax
import jax.numpy as jnp
from jax import lax
from jax.experimental import pallas as pl
from jax.experimental.pallas import tpu as pltpu

F32 = jnp.float32
BF16 = jnp.bfloat16

EPS = 1e-6
ROPE_THETA = 10000.0
WINDOW = 128
NEG_INF = -1e30

LANES = 128
SUBLANES = 8
BF16_ROWS = 16
VMEM_LIMIT_BYTES = 56 * 1024 * 1024


def _cparams(*sem):
    return pltpu.CompilerParams(dimension_semantics=sem, vmem_limit_bytes=VMEM_LIMIT_BYTES)


def _tile(n, pref, mult=SUBLANES):
    if n <= pref:
        return n
    t = (pref // mult) * mult
    while t > mult and n % t:
        t -= mult
    assert n % t == 0, (n, pref, mult)
    return t


def _rms(x, g):
    y = x * lax.rsqrt(jnp.mean(x * x, axis=-1, keepdims=True) + EPS)
    return y * g


def _dot(a, b):
    return jnp.dot(a, b, preferred_element_type=F32)


def _dot_nt(a, b):
    return lax.dot_general(a, b, (((1,), (1,)), ((), ())), preferred_element_type=F32)


def _lane_tile(x, n):
    return x if n == 1 else jnp.concatenate([x] * n, axis=1)


def _row_chunks(n_rows, chunk, fn):
    def body(c, carry):
        fn(pl.ds(pl.multiple_of(c * chunk, chunk), chunk))
        return carry
    lax.fori_loop(0, n_rows // chunk, body, 0, unroll=2)


def _mm_kernel(*refs, pre, post, nxt):
    it = iter(refs)
    x_ref, w_ref = next(it), next(it)
    gpre_ref = next(it) if pre else None
    gpost_ref, res_ref = (next(it), next(it)) if post else (None, None)
    gnext_ref = next(it) if nxt else None
    o_ref = next(it)
    x = x_ref[...]
    if pre:
        x = _rms(x.astype(F32), gpre_ref[...])
    acc = _dot(x.astype(BF16), w_ref[...])
    if post:
        acc = res_ref[...] + _rms(acc, gpost_ref[...])
    o_ref[...] = acc.astype(o_ref.dtype)
    if nxt:
        next(it)[...] = _rms(acc, gnext_ref[...]).astype(BF16)


def matmul_fused(x, w, *, g_pre=None, g_post=None, res=None, g_next=None, xcol=0, tm=256, tn=1536):
    M = x.shape[0]
    K, N = w.shape
    pre, post, nxt = g_pre is not None, g_post is not None, g_next is not None
    assert post or not nxt
    tm = _tile(M, tm, BF16_ROWS)
    tn = N if post else _tile(N, tn, LANES)
    grid = (N // tn, M // tm)
    vec = lambda n: pl.BlockSpec((1, n), lambda j, i: (0, 0))
    in_specs = [pl.BlockSpec((tm, K), lambda j, i: (i, xcol)),
                pl.BlockSpec((K, tn), lambda j, i: (0, j))]
    args = [x, w]
    if pre:
        in_specs.append(vec(K))
        args.append(g_pre.reshape(1, K).astype(F32))
    if post:
        in_specs += [vec(N), pl.BlockSpec((tm, N), lambda j, i: (i, 0))]
        args += [g_post.reshape(1, N).astype(F32), res]
    out_specs = [pl.BlockSpec((tm, tn), lambda j, i: (i, j))]
    out_shape = [jax.ShapeDtypeStruct((M, N), F32)]
    if nxt:
        in_specs.append(vec(N))
        args.append(g_next.reshape(1, N).astype(F32))
        out_specs.append(pl.BlockSpec((tm, N), lambda j, i: (i, 0)))
        out_shape.append(jax.ShapeDtypeStruct((M, N), BF16))
    outs = pl.pallas_call(
        functools.partial(_mm_kernel, pre=pre, post=post, nxt=nxt),
        grid=grid, in_specs=in_specs, out_specs=out_specs, out_shape=out_shape,
        compiler_params=_cparams("parallel", "parallel"),
    )(*args)
    return outs if nxt else outs[0]


FFN_ROW_CHUNK = 16
NORM_ROW_CHUNK = 16
HIST = SUBLANES
DOWN_PANEL = 512
FFN_STAGES = 2


def _ffn_kernel(*refs, seg, tiles_per_seq, nxt):
    (x_ref, gpre_ref, wg_ref, wu_ref, cwg_ref, cwu_ref, cbg_ref, cbu_ref, wd_ref,
     bufg_ref, bufu_ref, gpost_ref) = refs[:12]
    rest = list(refs[12:])
    gnext_ref = rest.pop(0) if nxt else None
    y_ref, sg_ref, su_ref = rest[:3]
    rest = rest[3:]
    h_ref = rest.pop(0) if nxt else None
    xn_ref, u_ref, hm_ref, carry_ref = rest
    i, j = pl.program_id(0), pl.program_id(1)
    nf = pl.num_programs(1)
    tm = x_ref.shape[0]
    tf = wg_ref.shape[1]
    R = FFN_ROW_CHUNK

    @pl.when(j == 0)
    def _():
        def norm_rows(rows):
            xn_ref[rows, :] = _rms(x_ref[rows, :], gpre_ref[...]).astype(BF16)
            y_ref[rows, :] = jnp.zeros((NORM_ROW_CHUNK, y_ref.shape[1]), F32)
        _row_chunks(tm, NORM_ROW_CHUNK, norm_rows)

    halves = ((wg_ref, cwg_ref, cbg_ref, bufg_ref, sg_ref), (wu_ref, cwu_ref, cbu_ref, bufu_ref, su_ref))
    rps = tm // FFN_STAGES

    if seg is None:
        for hidx, (_, _, _, buf_ref, _) in enumerate(halves):
            @pl.when((i % tiles_per_seq) == 0)
            def _(buf_ref=buf_ref, hidx=hidx):
                u_ref[hidx, 0:HIST, :] = buf_ref[0]

            @pl.when((i % tiles_per_seq) != 0)
            def _(hidx=hidx):
                u_ref[hidx, 0:HIST, :] = carry_ref[j, hidx]

    def up(a):
        for hidx, (w_ref, _, _, _, _) in enumerate(halves):
            u_ref[hidx, HIST + a:HIST + a + rps, :] = _dot(xn_ref[a:a + rps, :], w_ref[...])

    def gate(a):
        for r0 in range(a, a + rps, R):
            conv = []
            for hidx, (_, cw_ref, cb_ref, buf_ref, _) in enumerate(halves):
                if seg is None:
                    ext = u_ref[hidx, r0:r0 + R + HIST, :]
                    cur = ext[HIST:]
                    s1 = pltpu.roll(ext, 1, axis=0)[HIST:]
                    s2 = pltpu.roll(ext, 2, axis=0)[HIST:]
                else:
                    nsq = R // seg
                    cur = u_ref[hidx, r0 + HIST:r0 + HIST + R, :]
                    b = buf_ref[r0 // seg:r0 // seg + nsq]
                    b0 = jnp.broadcast_to(b[:, 0:1, :], (nsq, seg, tf)).reshape(R, tf)
                    b1 = jnp.broadcast_to(b[:, 1:2, :], (nsq, seg, tf)).reshape(R, tf)
                    t = lax.broadcasted_iota(jnp.int32, (R, 1), 0) & (seg - 1)
                    s1 = jnp.where(t == 0, b1, pltpu.roll(cur, 1, axis=0))
                    s2 = jnp.where(t == 0, b0, jnp.where(t == 1, b1, pltpu.roll(cur, 2, axis=0)))
                conv.append(cw_ref[0:1] * s2 + cw_ref[1:2] * s1 + cw_ref[2:3] * cur + cb_ref[...])
            gate_c, up_c = conv
            hm_ref[r0:r0 + R, :] = (gate_c * jax.nn.sigmoid(gate_c) * up_c).astype(BF16)

    def down(a):
        for n0 in range(0, y_ref.shape[1], DOWN_PANEL):
            n1 = min(n0 + DOWN_PANEL, y_ref.shape[1])
            y_ref[a:a + rps, n0:n1] += _dot(hm_ref[a:a + rps, :], wd_ref[:, n0:n1])

    for st in range(FFN_STAGES):
        up(st * rps)
    for hidx, (_, _, _, _, s_ref) in enumerate(halves):
        if seg is None:
            carry_ref[j, hidx] = u_ref[hidx, tm:tm + HIST, :]
            s_ref[0] = u_ref[hidx, tm:tm + HIST, :]
        else:
            s_ref[...] = u_ref[hidx, HIST:HIST + tm, :].reshape(tm // seg, seg, tf)[:, seg - 2:seg, :]
    for st in range(FFN_STAGES):
        gate(st * rps)
        down(st * rps)

    @pl.when(j == nf - 1)
    def _():
        def out_rows(rows):
            out = x_ref[rows, :] + _rms(y_ref[rows, :], gpost_ref[...])
            y_ref[rows, :] = out
            if nxt:
                h_ref[rows, :] = _rms(out, gnext_ref[...]).astype(BF16)
        _row_chunks(tm, NORM_ROW_CHUNK, out_rows)


def conv_ffn(x, g_pre, g_post, w_up, conv_w, conv_b, w_down, buf, *, seq_len, buf_block0=0, g_next=None,
             tm=1024, tf=512):
    M, D = x.shape
    F = w_down.shape[0]
    nseq = M // seq_len
    tf = _tile(F, tf, LANES)
    nf = F // tf
    nxt = g_next is not None
    if seq_len >= LANES:
        tm = _tile(seq_len, tm, BF16_ROWS)
        seg, tiles_per_seq = None, seq_len // tm
        buf = jnp.pad(buf, ((0, 0), (HIST - 2, 0), (0, 0)))
        buf_spec = lambda half: pl.BlockSpec((1, HIST, tf), lambda i, j: (buf_block0 * nseq + i // tiles_per_seq, 0, half * nf + j))
        st_spec = pl.BlockSpec((1, HIST, tf), lambda i, j: (i, 0, j))
        st_shape = jax.ShapeDtypeStruct((M // tm, HIST, F), F32)
    else:
        tm = _tile(M, tm, max(seq_len, FFN_ROW_CHUNK))
        seg, tiles_per_seq = seq_len, 1
        spt = tm // seq_len
        assert seq_len & (seq_len - 1) == 0 and FFN_ROW_CHUNK % seq_len == 0
        buf_spec = lambda half: pl.BlockSpec((spt, 2, tf), lambda i, j: (buf_block0 * (nseq // spt) + i, 0, half * nf + j))
        st_spec = pl.BlockSpec((spt, 2, tf), lambda i, j: (i, 0, j))
        st_shape = jax.ShapeDtypeStruct((nseq, 2, F), F32)
    assert tm % (FFN_STAGES * FFN_ROW_CHUNK) == 0
    row = lambda a: a.reshape(1, -1).astype(F32)
    vec = pl.BlockSpec((1, D), lambda i, j: (0, 0))
    once = dict(pipeline_mode=pl.Buffered(1))
    in_specs = [
        pl.BlockSpec((tm, D), lambda i, j: (i, 0), **once),
        vec,
        pl.BlockSpec((D, tf), lambda i, j: (0, j)),
        pl.BlockSpec((D, tf), lambda i, j: (0, nf + j)),
        pl.BlockSpec((3, tf), lambda i, j: (0, j)),
        pl.BlockSpec((3, tf), lambda i, j: (0, nf + j)),
        pl.BlockSpec((1, tf), lambda i, j: (0, j)),
        pl.BlockSpec((1, tf), lambda i, j: (0, nf + j)),
        pl.BlockSpec((tf, D), lambda i, j: (j, 0)),
        buf_spec(0), buf_spec(1), vec,
    ]
    args = [x, row(g_pre), w_up, w_up, conv_w, conv_w, row(conv_b), row(conv_b), w_down, buf, buf, row(g_post)]
    out_specs = [pl.BlockSpec((tm, D), lambda i, j: (i, 0), **once), st_spec, st_spec]
    out_shape = [jax.ShapeDtypeStruct((M, D), F32), st_shape, st_shape]
    if nxt:
        in_specs.append(vec)
        args.append(row(g_next))
        out_specs.append(pl.BlockSpec((tm, D), lambda i, j: (i, 0), **once))
        out_shape.append(jax.ShapeDtypeStruct((M, D), BF16))
    outs = pl.pallas_call(
        functools.partial(_ffn_kernel, seg=seg, tiles_per_seq=tiles_per_seq, nxt=nxt),
        grid=(M // tm, nf), in_specs=in_specs, out_specs=out_specs, out_shape=out_shape,
        scratch_shapes=[pltpu.VMEM((tm, D), BF16), pltpu.VMEM((2, HIST + tm, tf), F32),
                        pltpu.VMEM((tm, tf), BF16), pltpu.VMEM((nf, 2, HIST, tf), F32)],
        compiler_params=_cparams("arbitrary", "arbitrary"),
    )(*args)
    y, sg, su = outs[:3]
    state = jnp.concatenate([sg, su], axis=-1)
    if seg is None:
        state = state[tiles_per_seq - 1::tiles_per_seq, HIST - 2:]
    return y, state, (outs[3] if nxt else None)


def _rope(x, cos, sin_signed):
    W = x.shape[-1]
    lane = lax.broadcasted_iota(jnp.int32, (1, W), 1)
    swapped = jnp.where((lane & 63) < 32, pltpu.roll(x, W - 32, axis=1), pltpu.roll(x, 32, axis=1))
    return x * cos + swapped * sin_signed


def rope_tables(pos, width, rows):
    inv = ROPE_THETA ** (-jnp.arange(0, 64, 2, dtype=F32) / 64)
    ang = pos.astype(F32)[:, None] * inv[None, :]
    c, s = jnp.cos(ang), jnp.sin(ang)
    cos = jnp.tile(jnp.concatenate([c, c], axis=-1), (rows // pos.shape[0], width // 64))
    sin = jnp.tile(jnp.concatenate([-s, s], axis=-1), (rows // pos.shape[0], width // 64))
    return cos, sin


def _mla_pre_kernel(q_ref, ckv_ref, kpe_ref, gkv_ref, wuk_ref, cos_ref, sin_ref,
                    qcat_ref, kcat_ref, ckvn_ref, kper_ref, *, heads, nope, lora):
    cos, sin = cos_ref[...], sin_ref[...]
    q = q_ref[...]
    qpe = _rope(q[:, heads * nope:], cos, sin)
    for h in range(heads):
        qn = q[:, h * nope:(h + 1) * nope].astype(BF16)
        qcat_ref[h, :, 0:lora] = _dot(qn, wuk_ref[h]).astype(BF16)
        qcat_ref[h, :, lora:lora + LANES] = qpe[:, h * LANES:(h + 1) * LANES].astype(BF16)
    ckvn = _rms(ckv_ref[...], gkv_ref[...])
    kper = _rope(kpe_ref[...], cos[:, :LANES], sin[:, :LANES])
    ckvn_ref[...] = ckvn
    kper_ref[...] = kper
    kcat_ref[:, 0:lora] = ckvn.astype(BF16)
    kcat_ref[:, lora:lora + LANES] = kper.astype(BF16)


def mla_pre(q, z, g_kv, w_ukT, cos, sin, *, ckv_col, kpe_col, tm=256):
    M = q.shape[0]
    heads, nope, lora = w_ukT.shape
    tm = _tile(M, min(tm, cos.shape[0]), BF16_ROWS)
    ntab = cos.shape[0] // tm
    wq = q.shape[1]
    kw = lora + LANES
    return pl.pallas_call(
        functools.partial(_mla_pre_kernel, heads=heads, nope=nope, lora=lora),
        grid=(M // tm,),
        in_specs=[
            pl.BlockSpec((tm, wq), lambda i: (i, 0)),
            pl.BlockSpec((tm, lora), lambda i: (i, ckv_col)),
            pl.BlockSpec((tm, LANES), lambda i: (i, kpe_col)),
            pl.BlockSpec((1, lora), lambda i: (0, 0)),
            pl.BlockSpec((heads, nope, lora), lambda i: (0, 0, 0)),
            pl.BlockSpec((tm, heads * LANES), lambda i: (i % ntab, 0)),
            pl.BlockSpec((tm, heads * LANES), lambda i: (i % ntab, 0)),
        ],
        out_specs=[
            pl.BlockSpec((heads, tm, kw), lambda i: (0, i, 0)),
            pl.BlockSpec((tm, kw), lambda i: (i, 0)),
            pl.BlockSpec((tm, lora), lambda i: (i, 0)),
            pl.BlockSpec((tm, LANES), lambda i: (i, 0)),
        ],
        out_shape=[jax.ShapeDtypeStruct((heads, M, kw), BF16),
                   jax.ShapeDtypeStruct((M, kw), BF16),
                   jax.ShapeDtypeStruct((M, lora), F32),
                   jax.ShapeDtypeStruct((M, LANES), F32)],
        compiler_params=_cparams("parallel"),
    )(q, z, z, g_kv.reshape(1, lora).astype(F32), w_ukT, cos, sin)


def _mla_prompt_kernel(q_ref, k_ref, wuv_ref, y_ref, m_ref, l_ref, acc_ref, *, scale, lora, vdim):
    qi, ki = pl.program_id(1), pl.program_id(2)
    heads, tq, kw = q_ref.shape
    tk = k_ref.shape[0]

    @pl.when(ki == 0)
    def _():
        m_ref[...] = jnp.full_like(m_ref, NEG_INF)
        l_ref[...] = jnp.zeros_like(l_ref)
        acc_ref[...] = jnp.zeros_like(acc_ref)

    def step(masked):
        q = q_ref[...].reshape(heads * tq, kw)
        k = k_ref[...]
        s = _dot_nt(q, k) * scale
        if masked:
            qpos = qi * tq + (lax.broadcasted_iota(jnp.int32, (heads * tq, 1), 0) & (tq - 1))
            kpos = ki * tk + lax.broadcasted_iota(jnp.int32, (1, tk), 1)
            s = jnp.where(kpos <= qpos, s, NEG_INF)
        m_old = m_ref[...]
        m_new = jnp.maximum(m_old, jnp.max(s, axis=-1, keepdims=True))
        alpha = jnp.exp(m_old - m_new)
        p = jnp.exp(s - _lane_tile(m_new, tk // LANES))
        l_ref[...] = alpha * l_ref[...] + jnp.sum(p, axis=-1, keepdims=True)
        acc_ref[...] = _lane_tile(alpha, lora // LANES) * acc_ref[...] + _dot(p.astype(BF16), k[:, :lora])
        m_ref[...] = m_new

    needed = ki * tk <= qi * tq + tq - 1
    unmasked = ki * tk + tk - 1 <= qi * tq

    @pl.when(needed & unmasked)
    def _():
        step(False)

    @pl.when(needed & jnp.logical_not(unmasked))
    def _():
        step(True)

    @pl.when(ki == pl.num_programs(2) - 1)
    def _():
        o = (acc_ref[...] / _lane_tile(l_ref[...], lora // LANES)).astype(BF16)
        for h in range(heads):
            y_ref[:, h * vdim:(h + 1) * vdim] = _dot(o[h * tq:(h + 1) * tq], wuv_ref[:, h * vdim:(h + 1) * vdim])


def mla_attn_prompt(qcat, kcat, w_uv, *, batch, scale, tq=128, tk=512):
    heads, M, kw = qcat.shape
    S = M // batch
    lora, hv = w_uv.shape
    vdim = hv // heads
    tq = _tile(S, tq, BF16_ROWS)
    tk = _tile(S, tk, LANES)
    assert tq & (tq - 1) == 0 and tk % LANES == 0 and lora % LANES == 0
    nq, nk = S // tq, S // tk

    def kmap(b, qi, ki):
        return (b * nk + jnp.minimum(ki, (qi * tq + tq - 1) // tk), 0)

    return pl.pallas_call(
        functools.partial(_mla_prompt_kernel, scale=scale, lora=lora, vdim=vdim),
        grid=(batch, nq, nk),
        in_specs=[
            pl.BlockSpec((heads, tq, kw), lambda b, qi, ki: (0, b * nq + qi, 0)),
            pl.BlockSpec((tk, kw), kmap),
            pl.BlockSpec((lora, hv), lambda b, qi, ki: (0, 0)),
        ],
        out_specs=pl.BlockSpec((tq, hv), lambda b, qi, ki: (b * nq + qi, 0)),
        out_shape=jax.ShapeDtypeStruct((M, hv), F32),
        scratch_shapes=[pltpu.VMEM((heads * tq, LANES), F32), pltpu.VMEM((heads * tq, LANES), F32),
                        pltpu.VMEM((heads * tq, lora), F32)],
        compiler_params=_cparams("parallel", "parallel", "arbitrary"),
    )(qcat, kcat, w_uv)


def _mla_sample_kernel(pt_ref, q_ref, ckvn_ref, kper_ref, wuv_ref, poolc_ref, poolp_ref, y_ref,
                       kbuf_ref, pbuf_ref, knew_ref, sem_ref,
                       *, layer, scale, lora, rope, vdim, seq, n_pages, page, chunk):
    b = pl.program_id(0)
    slot = b % 2
    rows = q_ref.shape[1]

    def page_copies(seq_idx, sl, p):
        src_page = pt_ref[seq_idx, p]
        dst = pl.ds(pl.multiple_of(p * page, page), page)
        return (pltpu.make_async_copy(poolc_ref.at[layer, src_page], kbuf_ref.at[sl, dst], sem_ref.at[0, sl]),
                pltpu.make_async_copy(poolp_ref.at[layer, src_page], pbuf_ref.at[sl, dst], sem_ref.at[1, sl]))

    def for_pages(seq_idx, sl, action):
        def body(p, carry):
            for cp in page_copies(seq_idx, sl, p):
                action(cp)
            return carry
        lax.fori_loop(0, n_pages, body, 0)

    @pl.when(b == 0)
    def _():
        for_pages(0, 0, lambda cp: cp.start())

    @pl.when(b + 1 < pl.num_programs(0))
    def _():
        for_pages(b + 1, 1 - slot, lambda cp: cp.start())

    for_pages(b, slot, lambda cp: cp.wait())

    q = q_ref[0]
    q_lat, q_pe = q[:, :lora], q[:, lora:lora + rope]
    scores, vals = [], []
    for c0 in range(0, n_pages * page, chunk):
        kc = kbuf_ref[slot, c0:c0 + chunk, :].astype(BF16)
        kp = pbuf_ref[slot, c0:c0 + chunk, :].astype(BF16)
        scores.append((_dot_nt(q_lat, kc) + _dot_nt(q_pe, kp)) * scale)
        vals.append(kc)
    knew_ref[...] = jnp.zeros_like(knew_ref)
    knew_ref[0:seq, 0:lora] = ckvn_ref[...]
    knew_ref[0:seq, lora:lora + LANES] = kper_ref[...]
    kn = knew_ref[...].astype(BF16)
    qidx = lax.broadcasted_iota(jnp.int32, (rows, 1), 0) & (seq - 1)
    kidx = lax.broadcasted_iota(jnp.int32, (1, page), 1)
    scores.append(jnp.where(kidx <= qidx, _dot_nt(q, kn) * scale, NEG_INF))
    vals.append(kn[:, :lora])
    m = functools.reduce(jnp.maximum, [jnp.max(s, axis=-1, keepdims=True) for s in scores])
    probs = [jnp.exp(s - m) for s in scores]
    l = functools.reduce(jnp.add, [jnp.sum(p, axis=-1, keepdims=True) for p in probs])
    acc = functools.reduce(jnp.add, [_dot(p.astype(BF16), v) for p, v in zip(probs, vals)])
    o = (acc / l).astype(BF16)
    full = _dot(o, wuv_ref[...])
    for h in range(rows // seq):
        y_ref[:, h * vdim:(h + 1) * vdim] = full[h * seq:(h + 1) * seq, h * vdim:(h + 1) * vdim]


def mla_attn_sample(qs, pool_ckv, pool_kpe, layer, page_table, ckv_n, kpe_r, w_uv, *, scale, chunk_pages=16):
    B, rows, kw = qs.shape
    _, _, page, lora = pool_ckv.shape
    rope = pool_kpe.shape[-1]
    n_pages = page_table.shape[1]
    chunk = _tile(n_pages, chunk_pages, 1) * page
    hv = w_uv.shape[1]
    seq = ckv_n.shape[0] // B
    vdim = hv // (rows // seq)
    assert seq & (seq - 1) == 0 and seq <= page
    return pl.pallas_call(
        functools.partial(_mla_sample_kernel, layer=layer, scale=scale, lora=lora, rope=rope, vdim=vdim, seq=seq,
                          n_pages=n_pages, page=page, chunk=chunk),
        grid_spec=pltpu.PrefetchScalarGridSpec(
            num_scalar_prefetch=1, grid=(B,),
            in_specs=[pl.BlockSpec((1, rows, kw), lambda b, pt: (b, 0, 0)),
                      pl.BlockSpec((seq, lora), lambda b, pt: (b, 0)),
                      pl.BlockSpec((seq, LANES), lambda b, pt: (b, 0)),
                      pl.BlockSpec((lora, hv), lambda b, pt: (0, 0)),
                      pl.BlockSpec(memory_space=pl.ANY),
                      pl.BlockSpec(memory_space=pl.ANY)],
            out_specs=pl.BlockSpec((seq, hv), lambda b, pt: (b, 0)),
            scratch_shapes=[pltpu.VMEM((2, n_pages * page, lora), F32),
                            pltpu.VMEM((2, n_pages * page, rope), F32),
                            pltpu.VMEM((page, kw), F32),
                            pltpu.SemaphoreType.DMA((2, 2))]),
        out_shape=jax.ShapeDtypeStruct((B * seq, hv), F32),
        compiler_params=_cparams("arbitrary"),
    )(page_table, qs, ckv_n, kpe_r, w_uv, pool_ckv, pool_kpe)


def _dwconv_kernel(*refs, mode, taps, hist):
    if mode == "conf":
        a_ref, b_ref, st_ref, w_ref, cb_ref, lg_ref, lb_ref, y_ref, so_ref, buf_ref, c_ref = refs
    else:
        gate_ref, a_ref, b_ref, st_ref, w_ref, y_ref, so_ref, buf_ref, c_ref = refs
    t = pl.program_id(1)
    tt, C = a_ref.shape

    @pl.when(t == 0)
    def _():
        buf_ref[0:hist] = st_ref[0]

    if mode == "conf":
        u = a_ref[...] * jax.nn.sigmoid(b_ref[...])
    else:
        u = a_ref[...] * b_ref[...]
    buf_ref[hist:hist + tt] = u
    rc = min(tt, 64)
    for c0 in range(0, C, LANES):
        for r0 in range(0, tt, rc):
            acc = None
            for k in range(taps):
                term = (w_ref[k:k + 1, c0:c0 + LANES]
                        * buf_ref[r0 + hist - (taps - 1) + k:r0 + hist - (taps - 1) + k + rc, c0:c0 + LANES])
                acc = term if acc is None else acc + term
            c_ref[r0:r0 + rc, c0:c0 + LANES] = acc
    c = c_ref[...]
    if mode == "conf":
        c = c + cb_ref[...]
        mu = jnp.mean(c, axis=-1, keepdims=True)
        d = c - mu
        var = jnp.mean(d * d, axis=-1, keepdims=True)
        yn = d * lax.rsqrt(var + EPS) * lg_ref[...] + lb_ref[...]
        y_ref[...] = yn * jax.nn.sigmoid(yn)
    else:
        y_ref[...] = gate_ref[...] * c
    so_ref[0] = buf_ref[tt:tt + hist]
    if tt >= hist:
        buf_ref[0:hist] = buf_ref[tt:tt + hist]


def dwconv(z, state, w, *, mode, cols, batch, extra=(), tt=128):
    M = z.shape[0]
    taps, C = w.shape
    S = M // batch
    hist = -(-(taps - 1) // SUBLANES) * SUBLANES
    tt = _tile(S, tt)
    nt = S // tt
    assert nt == 1 or tt >= hist
    st = jnp.pad(state.astype(F32), ((0, 0), (hist - (taps - 1), 0), (0, 0)))
    colspec = lambda c: pl.BlockSpec((tt, C), lambda b, t: (b * nt + t, c))
    vec = pl.BlockSpec((1, C), lambda b, t: (0, 0))
    in_specs = [colspec(c) for c in cols]
    in_specs += [pl.BlockSpec((1, hist, C), lambda b, t: (b, 0, 0)), pl.BlockSpec((taps, C), lambda b, t: (0, 0))]
    in_specs += [vec] * len(extra)
    y, so = pl.pallas_call(
        functools.partial(_dwconv_kernel, mode=mode, taps=taps, hist=hist),
        grid=(batch, nt),
        in_specs=in_specs,
        out_specs=[pl.BlockSpec((tt, C), lambda b, t: (b * nt + t, 0)),
                   pl.BlockSpec((1, hist, C), lambda b, t: (b, 0, 0))],
        out_shape=[jax.ShapeDtypeStruct((M, C), F32), jax.ShapeDtypeStruct((batch, hist, C), F32)],
        scratch_shapes=[pltpu.VMEM((hist + tt, C), F32), pltpu.VMEM((tt, C), F32)],
        compiler_params=_cparams("parallel", "arbitrary"),
    )(*([z] * len(cols)), st, w.astype(F32), *[e.reshape(1, C).astype(F32) for e in extra])
    return y, so[:, hist - (taps - 1):]


def _swa_rope_kernel(q_ref, k_ref, cos_ref, sin_ref, qo_ref, ko_ref):
    cos, sin = cos_ref[...], sin_ref[...]
    kw = k_ref.shape[1]
    qo_ref[...] = _rope(q_ref[...], cos, sin)
    ko_ref[...] = _rope(k_ref[...], cos[:, :kw], sin[:, :kw])


def swa_rope(z, cos, sin, *, q_col, k_col, qw, kw, tm=256):
    M = z.shape[0]
    tm = _tile(M, min(tm, cos.shape[0]))
    ntab = cos.shape[0] // tm
    return pl.pallas_call(
        _swa_rope_kernel,
        grid=(M // tm,),
        in_specs=[pl.BlockSpec((tm, qw), lambda i: (i, q_col)),
                  pl.BlockSpec((tm, kw), lambda i: (i, k_col)),
                  pl.BlockSpec((tm, qw), lambda i: (i % ntab, 0)),
                  pl.BlockSpec((tm, qw), lambda i: (i % ntab, 0))],
        out_specs=[pl.BlockSpec((tm, qw), lambda i: (i, 0)), pl.BlockSpec((tm, kw), lambda i: (i, 0))],
        out_shape=[jax.ShapeDtypeStruct((M, qw), F32), jax.ShapeDtypeStruct((M, kw), F32)],
        compiler_params=_cparams("parallel"),
    )(z, z, cos, sin)


def _sink_softmax_pv(s, sink, v):
    m = jnp.maximum(jnp.max(s, axis=-1, keepdims=True), sink)
    p = jnp.exp(s - m)
    denom = jnp.sum(p, axis=-1, keepdims=True) + jnp.exp(sink - m)
    return _dot(p.astype(BF16), v) / denom


def _swa_prompt_kernel(sink_ref, q_ref, kc_ref, kp_ref, vc_ref, vp_ref, y_ref, *, scale, hd, group):
    n = pl.program_id(1)
    W = q_ref.shape[0]
    kvh = kc_ref.shape[1] // hd
    qi = lax.broadcasted_iota(jnp.int32, (W, 1), 0)
    si = lax.broadcasted_iota(jnp.int32, (1, 2 * W), 1)
    rel = qi + W - si
    valid = (rel >= 0) & (rel < W) & (n * W + si - W >= 0)
    q = q_ref[...].astype(BF16)
    kk = jnp.concatenate([kp_ref[...], kc_ref[...]], axis=0).astype(BF16)
    vv = jnp.concatenate([vp_ref[...], vc_ref[...]], axis=0).astype(BF16)
    outs = []
    for kv in range(kvh):
        k_h = kk[:, kv * hd:(kv + 1) * hd]
        v_h = vv[:, kv * hd:(kv + 1) * hd]
        for g in range(group):
            h = kv * group + g
            s = _dot_nt(q[:, h * hd:(h + 1) * hd], k_h) * scale
            s = jnp.where(valid, s, NEG_INF)
            outs.append(_sink_softmax_pv(s, sink_ref[h], v_h))
    y_ref[...] = jnp.concatenate(outs, axis=-1)


def swa_prompt(q, k, z, sinks, *, v_col, batch, scale, hd):
    M, qw = q.shape
    kw = k.shape[1]
    S = M // batch
    W = WINDOW
    nb = S // W
    group = qw // kw
    cur = lambda b, n: b * nb + n
    prev = lambda b, n: b * nb + jnp.maximum(n - 1, 0)
    return pl.pallas_call(
        functools.partial(_swa_prompt_kernel, scale=scale, hd=hd, group=group),
        grid=(batch, nb),
        in_specs=[pl.BlockSpec(memory_space=pltpu.SMEM),
                  pl.BlockSpec((W, qw), lambda b, n: (cur(b, n), 0)),
                  pl.BlockSpec((W, kw), lambda b, n: (cur(b, n), 0)),
                  pl.BlockSpec((W, kw), lambda b, n: (prev(b, n), 0)),
                  pl.BlockSpec((W, kw), lambda b, n: (cur(b, n), v_col)),
                  pl.BlockSpec((W, kw), lambda b, n: (prev(b, n), v_col))],
        out_specs=pl.BlockSpec((W, qw), lambda b, n: (cur(b, n), 0)),
        out_shape=jax.ShapeDtypeStruct((M, qw), F32),
        compiler_params=_cparams("parallel", "parallel"),
    )(sinks.astype(F32), q, k, k, z, z)


def _swa_sample_kernel(sink_ref, q_ref, kb_ref, vb_ref, kn_ref, vn_ref, y_ref, ko_ref, vo_ref,
                       ks_ref, vs_ref, *, scale, hd, group, first_pos, seq):
    nseq, nbuf = kb_ref.shape[0], kb_ref.shape[1]
    npad = ks_ref.shape[0]
    kvh = kn_ref.shape[1] // hd
    qi = lax.broadcasted_iota(jnp.int32, (group * seq, 1), 0) & (seq - 1)
    ji = lax.broadcasted_iota(jnp.int32, (1, npad), 1)
    rel = qi + nbuf - ji
    valid = (rel >= 0) & (rel < WINDOW) & (first_pos + ji >= 0)
    sinks = [jnp.concatenate([jnp.full((seq, 1), sink_ref[kv * group + g], F32) for g in range(group)], axis=0)
             for kv in range(kvh)]
    for n in range(nseq):
        rows = slice(n * seq, (n + 1) * seq)
        for s_ref, b_ref, n_ref, o_ref in ((ks_ref, kb_ref, kn_ref, ko_ref), (vs_ref, vb_ref, vn_ref, vo_ref)):
            s_ref[0:nbuf] = b_ref[n]
            s_ref[nbuf:nbuf + seq] = n_ref[rows, :]
            s_ref[nbuf + seq:npad] = jnp.zeros((npad - nbuf - seq, s_ref.shape[1]), F32)
            o_ref[n] = s_ref[seq:seq + nbuf]
        kk = ks_ref[...].astype(BF16)
        vv = vs_ref[...].astype(BF16)
        q = q_ref[rows, :]
        outs = []
        for kv in range(kvh):
            q4 = jnp.concatenate([q[:, (kv * group + g) * hd:(kv * group + g + 1) * hd] for g in range(group)], axis=0)
            s = _dot_nt(q4.astype(BF16), kk[:, kv * hd:(kv + 1) * hd]) * scale
            s = jnp.where(valid, s, NEG_INF)
            o = _sink_softmax_pv(s, sinks[kv], vv[:, kv * hd:(kv + 1) * hd])
            outs += [o[g * seq:(g + 1) * seq] for g in range(group)]
        y_ref[rows, :] = jnp.concatenate(outs, axis=-1)


def swa_sample(q, k_new, z, buf_k, buf_v, buf_block0, sinks, *, v_col, batch, scale, hd, past_len, nseq=8):
    M, qw = q.shape
    _, nbuf, kw = buf_k.shape
    seq = M // batch
    group = qw // kw
    npad = 2 * nbuf
    nseq = _tile(batch, nseq, 1)
    steps = batch // nseq
    assert seq & (seq - 1) == 0 and nbuf + seq <= npad and nbuf == WINDOW
    row = lambda b: (b, 0)
    bufspec = pl.BlockSpec((nseq, nbuf, kw), lambda b: (buf_block0 * steps + b, 0, 0))
    outbuf = pl.BlockSpec((nseq, nbuf, kw), lambda b: (b, 0, 0))
    return pl.pallas_call(
        functools.partial(_swa_sample_kernel, scale=scale, hd=hd, group=group, first_pos=past_len - nbuf, seq=seq),
        grid=(steps,),
        in_specs=[pl.BlockSpec(memory_space=pltpu.SMEM),
                  pl.BlockSpec((nseq * seq, qw), row),
                  bufspec, bufspec,
                  pl.BlockSpec((nseq * seq, kw), row),
                  pl.BlockSpec((nseq * seq, kw), lambda b: (b, v_col))],
        out_specs=[pl.BlockSpec((nseq * seq, qw), row), outbuf, outbuf],
        out_shape=[jax.ShapeDtypeStruct((M, qw), F32),
                   jax.ShapeDtypeStruct((batch, nbuf, kw), F32),
                   jax.ShapeDtypeStruct((batch, nbuf, kw), F32)],
        scratch_shapes=[pltpu.VMEM((npad, kw), F32), pltpu.VMEM((npad, kw), F32)],
        compiler_params=_cparams("parallel"),
    )(sinks.astype(F32), q, buf_k, buf_v, k_new, z)


def _mem_attn_kernel(q_ref, k_ref, v_ref, o_ref, *, scale, hd, nseq):
    heads = q_ref.shape[1] // hd
    tq = q_ref.shape[0] // nseq
    mt = k_ref.shape[0] // nseq
    for n in range(nseq):
        q = q_ref[n * tq:(n + 1) * tq, :].astype(BF16)
        k = k_ref[n * mt:(n + 1) * mt, :].astype(BF16)
        v = v_ref[n * mt:(n + 1) * mt, :].astype(BF16)
        outs = []
        for h in range(heads):
            sl = slice(h * hd, (h + 1) * hd)
            s = _dot_nt(q[:, sl], k[:, sl]) * scale
            m = jnp.max(s, axis=-1, keepdims=True)
            p = jnp.exp(s - m)
            p = p / jnp.sum(p, axis=-1, keepdims=True)
            outs.append(_dot(p.astype(BF16), v[:, sl]))
        o_ref[n * tq:(n + 1) * tq, :] = jnp.concatenate(outs, axis=-1)


def mem_attn(q, k_arr, v_arr, *, k_col, v_col, kv_block0, batch, mem_tokens, scale, hd, tq=256, nseq=8):
    M, w = q.shape
    S = M // batch
    tq = _tile(S, tq)
    nq = S // tq
    nseq = _tile(batch, nseq, 1) if nq == 1 else 1
    nb = batch // nseq
    return pl.pallas_call(
        functools.partial(_mem_attn_kernel, scale=scale, hd=hd, nseq=nseq),
        grid=(nb, nq),
        in_specs=[pl.BlockSpec((nseq * tq, w), lambda b, i: (b * nq + i, 0)),
                  pl.BlockSpec((nseq * mem_tokens, w), lambda b, i: (kv_block0 * nb + b, k_col)),
                  pl.BlockSpec((nseq * mem_tokens, w), lambda b, i: (kv_block0 * nb + b, v_col))],
        out_specs=pl.BlockSpec((nseq * tq, w), lambda b, i: (b * nq + i, 0)),
        out_shape=jax.ShapeDtypeStruct((M, w), F32),
        compiler_params=_cparams("parallel", "parallel"),
    )(q, k_arr, v_arr)


def _prep_weights(p):
    dims = p["dims"]
    H, nope, rope, lora, qlora = dims["mla_heads"], dims["nope"], dims["rope"], dims["lora"], dims["qlora"]
    w = {}
    wi = p["w_even_in"]
    n_in = wi.shape[-1]
    pad = jnp.zeros(wi.shape[:2] + (-(-(n_in + LANES - rope) // 512) * 512 - n_in,), wi.dtype)
    w["even_in"] = jnp.concatenate(
        [wi[..., qlora + lora + rope:], wi[..., :qlora + lora + rope], pad], axis=-1).astype(BF16)
    wq = p["w_q_b"].reshape(p["w_q_b"].shape[0], qlora, H, nope + rope)
    wq_pe = jnp.pad(wq[..., nope:], ((0, 0), (0, 0), (0, 0), (0, LANES - rope)))
    w["q_b"] = jnp.concatenate([wq[..., :nope].reshape(-1, qlora, H * nope),
                                wq_pe.reshape(-1, qlora, H * LANES)], axis=-1).astype(BF16)
    w["ukT"] = jnp.transpose(p["w_uk"], (0, 2, 3, 1)).astype(BF16)
    w["uv"] = p["w_uv"].reshape(p["w_uv"].shape[0], lora, -1).astype(BF16)
    for name in ("w_even_out", "w_odd_in", "w_odd_out", "w_mem_q", "w_mem_kv", "w_mem_o", "w_ffn_up", "w_ffn_down"):
        w[name] = p[name].astype(BF16)
    return w


def _trunk(x, pos, p, w, mem, past, page_table):
    dims = p["dims"]
    B, S, D = x.shape
    M = B * S
    H, nope, rope, lora, qlora = dims["mla_heads"], dims["nope"], dims["rope"], dims["lora"], dims["qlora"]
    conf_ch, sc_ch = dims["conf_ch"], dims["sc_ch"]
    swa_h, swa_kv, swa_hd = dims["swa_heads"], dims["swa_kv"], dims["swa_hd"]
    mem_h, mem_hd = dims["mem_heads"], dims["mem_hd"]
    mem_w = mem_h * mem_hd
    ff = dims["d_ff"]
    depth = p["norm_gains"].shape[0]
    prompt = past is None
    x = x.reshape(M, D)
    tab_rows = S if prompt else max(S, min(M, 256))
    cos, sin = rope_tables(pos, max(H * LANES, swa_h * swa_hd), tab_rows)
    new = {k: [] for k in ("mla_ckv", "mla_kpe", "swa_k", "swa_v", "mem_k", "mem_v", "conf_conv", "sc_conv", "ffn_conv")}
    if not prompt:
        mem_k_all = past["mem_k"].reshape(-1, mem_w)
        mem_v_all = past["mem_v"].reshape(-1, mem_w)
        swa_k_all = past["swa_k"].reshape((-1,) + past["swa_k"].shape[2:3] + (swa_kv * swa_hd,))
        swa_v_all = past["swa_v"].reshape(swa_k_all.shape)
        ffn_all = past["ffn_conv"].reshape((-1,) + past["ffn_conv"].shape[2:])
    h = None
    for layer in range(depth):
        g = p["norm_gains"][layer]
        w_in = w["even_in"][layer // 2] if layer % 2 == 0 else w["w_odd_in"][layer // 2]
        z = matmul_fused(x, w_in, g_pre=g[0]) if h is None else matmul_fused(h, w_in, tm=512)
        if layer % 2 == 0:
            i = layer // 2
            glu_w = 2 * conf_ch
            q = matmul_fused(z, w["q_b"][i], g_pre=p["g_q_a"][i], xcol=glu_w // qlora)
            qcat, kcat, ckv_n, kpe_r = mla_pre(
                q, z, p["g_kv_a"][i], w["ukT"][i], cos[:, :H * LANES], sin[:, :H * LANES],
                ckv_col=(glu_w + qlora) // lora, kpe_col=(glu_w + qlora + lora) // LANES)
            scale = (nope + rope) ** -0.5
            if prompt:
                y_mla = mla_attn_prompt(qcat, kcat, w["uv"][i], batch=B, scale=scale)
                conf_state = jnp.zeros((B, p["conf_dw_w"].shape[1] - 1, conf_ch), F32)
            else:
                qs = qcat.reshape(H, B, S, lora + LANES).transpose(1, 0, 2, 3).reshape(B, H * S, lora + LANES)
                y_mla = mla_attn_sample(qs, past["mla_ckv"], past["mla_kpe"], i, page_table,
                                        ckv_n, kpe_r, w["uv"][i], scale=scale)
                conf_state = past["conf_conv"][i]
            c, conf_new = dwconv(z, conf_state, p["conf_dw_w"][i], mode="conf", cols=(0, 1), batch=B,
                                 extra=(p["conf_dw_b"][i], p["conf_ln_g"][i], p["conf_ln_b"][i]))
            x, hq = matmul_fused(jnp.concatenate([y_mla, c], axis=-1), w["w_even_out"][i],
                                 g_post=g[1], res=x, g_next=g[2])
            new["mla_ckv"].append(ckv_n.reshape(B, S, lora))
            new["mla_kpe"].append(kpe_r[:, :rope].reshape(B, S, rope))
            new["conf_conv"].append(conf_new)
        else:
            j = layer // 2
            qw, kw = swa_h * swa_hd, swa_kv * swa_hd
            sc_state = jnp.zeros((B, p["sc_conv_w"].shape[1] - 1, sc_ch), F32) if prompt else past["sc_conv"][j]
            y_sc, sc_new = dwconv(z, sc_state, p["sc_conv_w"][j], mode="sc", cols=(0, 1, 2), batch=B)
            q_r, k_r = swa_rope(z, cos[:, :qw], sin[:, :qw], q_col=3 * sc_ch // qw, k_col=(3 * sc_ch + qw) // kw,
                                qw=qw, kw=kw)
            v_col = (3 * sc_ch + qw + kw) // kw
            scale = swa_hd ** -0.5
            if prompt:
                y_att = swa_prompt(q_r, k_r, z, p["swa_sinks"][j], v_col=v_col, batch=B, scale=scale, hd=swa_hd)
                v = z[:, v_col * kw:(v_col + 1) * kw]
                nbuf = min(WINDOW, S)
                nk = k_r.reshape(B, S, swa_kv, swa_hd)[:, S - nbuf:]
                nv = v.reshape(B, S, swa_kv, swa_hd)[:, S - nbuf:]
            else:
                nbuf = swa_k_all.shape[1]
                y_att, nk, nv = swa_sample(q_r, k_r, z, swa_k_all, swa_v_all, j, p["swa_sinks"][j], v_col=v_col,
                                           batch=B, scale=scale, hd=swa_hd, past_len=past["past_len"])
                nk = nk.reshape(B, nbuf, swa_kv, swa_hd)
                nv = nv.reshape(B, nbuf, swa_kv, swa_hd)
            x, hq = matmul_fused(jnp.concatenate([y_sc, y_att], axis=-1), w["w_odd_out"][j],
                                 g_post=g[1], res=x, g_next=g[2])
            new["swa_k"].append(nk)
            new["swa_v"].append(nv)
            new["sc_conv"].append(sc_new)
        qm = matmul_fused(hq, w["w_mem_q"][layer], tm=512)
        if prompt:
            mem_tokens = mem.shape[1]
            kv = matmul_fused(mem.reshape(-1, D), w["w_mem_kv"][layer])
            o = mem_attn(qm, kv, kv, k_col=0, v_col=1, kv_block0=0, batch=B, mem_tokens=mem_tokens,
                         scale=mem_hd ** -0.5, hd=mem_hd)
            kv = kv.reshape(B, mem_tokens, 2, mem_h, mem_hd)
            new["mem_k"].append(kv[:, :, 0])
            new["mem_v"].append(kv[:, :, 1])
        else:
            o = mem_attn(qm, mem_k_all, mem_v_all, k_col=0, v_col=0, kv_block0=layer, batch=B,
                         mem_tokens=past["mem_k"].shape[2], scale=mem_hd ** -0.5, hd=mem_hd)
        x = matmul_fused(o, w["w_mem_o"][layer], g_post=g[3], res=x)
        g_next = p["norm_gains"][layer + 1][0] if layer + 1 < depth else None
        if prompt:
            ffn_buf, block0 = jnp.zeros((B, 2, 2 * ff), F32), 0
        else:
            ffn_buf, block0 = ffn_all, layer
        x, ffn_new, h = conv_ffn(x, g[4], g[5], w["w_ffn_up"][layer], p["ffn_conv_w"][layer], p["ffn_conv_b"][layer],
                                 w["w_ffn_down"][layer], ffn_buf, seq_len=S, buf_block0=block0, g_next=g_next)
        new["ffn_conv"].append(ffn_new)
    return x.reshape(B, S, D), {k: jnp.stack(v) for k, v in new.items() if v}


def kernel(x_prompt, x_sample, cache_mla_ckv, cache_mla_kpe, cache_swa_k, cache_swa_v, cache_mem_k, cache_mem_v,
           state_conf_conv, state_sc_conv, state_ffn_conv, page_table, mem_prompt, norm_gains, w_even_in, g_q_a,
           w_q_b, g_kv_a, w_uk, w_uv, conf_dw_w, conf_dw_b, conf_ln_g, conf_ln_b, w_even_out, w_odd_in, sc_conv_w,
           swa_sinks, w_odd_out, w_mem_q, w_mem_kv, w_mem_o, w_ffn_up, ffn_conv_w, ffn_conv_b, w_ffn_down):
    lora, heads, nope = w_uk.shape[1], w_uk.shape[2], w_uk.shape[3]
    rope = cache_mla_kpe.shape[-1]
    dims = dict(mla_heads=heads, nope=nope, rope=rope, lora=lora, qlora=g_q_a.shape[1],
                conf_ch=conf_dw_w.shape[2], sc_ch=sc_conv_w.shape[2],
                swa_heads=swa_sinks.shape[1], swa_kv=cache_swa_k.shape[3], swa_hd=cache_swa_k.shape[4],
                mem_heads=cache_mem_k.shape[3], mem_hd=cache_mem_k.shape[4], d_ff=w_ffn_down.shape[1])
    assert rope == 64 and dims["swa_hd"] == 64, "rotary helper assumes 64-wide rotary heads"
    p = dict(dims=dims, norm_gains=norm_gains, w_even_in=w_even_in, g_q_a=g_q_a, w_q_b=w_q_b, g_kv_a=g_kv_a,
             w_uk=w_uk, w_uv=w_uv, conf_dw_w=conf_dw_w, conf_dw_b=conf_dw_b, conf_ln_g=conf_ln_g,
             conf_ln_b=conf_ln_b, w_even_out=w_even_out, w_odd_in=w_odd_in, sc_conv_w=sc_conv_w,
             swa_sinks=swa_sinks, w_odd_out=w_odd_out, w_mem_q=w_mem_q, w_mem_kv=w_mem_kv, w_mem_o=w_mem_o,
             w_ffn_up=w_ffn_up, ffn_conv_w=ffn_conv_w, ffn_conv_b=ffn_conv_b, w_ffn_down=w_ffn_down)
    w = _prep_weights(p)
    pos_p = jnp.arange(x_prompt.shape[1], dtype=jnp.int32)
    y_prompt, new_p = _trunk(x_prompt, pos_p, p, w, mem_prompt, None, None)
    past_len = page_table.shape[1] * cache_mla_ckv.shape[2]
    pos_s = past_len + jnp.arange(x_sample.shape[1], dtype=jnp.int32)
    past = dict(mla_ckv=cache_mla_ckv, mla_kpe=cache_mla_kpe, swa_k=cache_swa_k, swa_v=cache_swa_v,
                mem_k=cache_mem_k, mem_v=cache_mem_v, conf_conv=state_conf_conv, sc_conv=state_sc_conv,
                ffn_conv=state_ffn_conv, past_len=past_len)
    y_sample, new_s = _trunk(x_sample, pos_s, p, w, None, past, page_table)
    return (y_prompt, y_sample,
            new_p["mla_ckv"], new_p["mla_kpe"], new_p["swa_k"], new_p["swa_v"], new_p["mem_k"], new_p["mem_v"],
            new_p["conf_conv"], new_p["sc_conv"], new_p["ffn_conv"],
            new_s["mla_ckv"], new_s["mla_kpe"], new_s["swa_k"], new_s["swa_v"],
            new_s["conf_conv"], new_s["sc_conv"], new_s["ffn_conv"])
```

```python
import functools

import jax
import jax.numpy as jnp
from jax import lax
from jax.experimental import pallas as pl
from jax.experimental.pallas import tpu as pltpu

F32 = jnp.float32
BF16 = jnp.bfloat16

EPS = 1e-6
ROPE_THETA = 10000.0
WINDOW = 128
NEG_INF = -1e30

LANES = 128
SUBLANES = 8
BF16_ROWS = 16
VMEM_LIMIT_BYTES = 56 * 1024 * 1024


def _cparams(*sem):
    return pltpu.CompilerParams(dimension_semantics=sem, vmem_limit_bytes=VMEM_LIMIT_BYTES)


def _tile(n, pref, mult=SUBLANES):
    if n <= pref:
        return n
    t = (pref // mult) * mult
    while t > mult and n % t:
        t -= mult
    assert n % t == 0, (n, pref, mult)
    return t


def _rms(x, g):
    y = x * lax.rsqrt(jnp.mean(x * x, axis=-1, keepdims=True) + EPS)
    return y * g


def _dot(a, b):
    return jnp.dot(a, b, preferred_element_type=F32)


def _dot_nt(a, b):
    return lax.dot_general(a, b, (((1,), (1,)), ((), ())), preferred_element_type=F32)


def _div_pow2(x, n):
    assert n & (n - 1) == 0, n
    return x >> (n.bit_length() - 1)


def _lane_tile(x, n):
    return x if n == 1 else jnp.concatenate([x] * n, axis=1)


def _mm_kernel(*refs, pre, post, nxt, two):
    it = iter(refs)
    x_ref, w_ref = next(it), next(it)
    x2_ref = next(it) if two else None
    gpre_ref = next(it) if pre else None
    gpost_ref, res_ref = (next(it), next(it)) if post else (None, None)
    gnext_ref = next(it) if nxt else None
    o_ref = next(it)
    x = x_ref[...]
    if pre:
        x = _rms(x.astype(F32), gpre_ref[...])
    k1 = x.shape[1]
    acc = _dot(x.astype(BF16), w_ref[0:k1, :])
    if two:
        acc = acc + _dot(x2_ref[...].astype(BF16), w_ref[k1:, :])
    if post:
        acc = res_ref[...] + _rms(acc, gpost_ref[...])
    o_ref[...] = acc.astype(o_ref.dtype)
    if nxt:
        next(it)[...] = _rms(acc, gnext_ref[...]).astype(BF16)


def matmul_fused(x, w, *, x2=None, g_pre=None, g_post=None, res=None, g_next=None, xcol=0, tm=256, tn=1536):
    M = x.shape[0]
    K, N = w.shape
    pre, post, nxt, two = g_pre is not None, g_post is not None, g_next is not None, x2 is not None
    assert post or not nxt
    assert not (two and (pre or xcol))
    tm = _tile(M, tm, BF16_ROWS)
    tn = N if post else _tile(N, tn, LANES)
    grid = (N // tn, M // tm)
    vec = lambda n: pl.BlockSpec((1, n), lambda j, i: (0, 0))
    k1 = K - x2.shape[1] if two else K
    in_specs = [pl.BlockSpec((tm, k1), lambda j, i: (i, xcol)),
                pl.BlockSpec((K, tn), lambda j, i: (0, j))]
    args = [x, w]
    if two:
        in_specs.append(pl.BlockSpec((tm, K - k1), lambda j, i: (i, 0)))
        args.append(x2)
    if pre:
        in_specs.append(vec(K))
        args.append(g_pre.reshape(1, K).astype(F32))
    if post:
        in_specs += [vec(N), pl.BlockSpec((tm, N), lambda j, i: (i, 0))]
        args += [g_post.reshape(1, N).astype(F32), res]
    out_specs = [pl.BlockSpec((tm, tn), lambda j, i: (i, j))]
    out_shape = [jax.ShapeDtypeStruct((M, N), F32)]
    if nxt:
        in_specs.append(vec(N))
        args.append(g_next.reshape(1, N).astype(F32))
        out_specs.append(pl.BlockSpec((tm, N), lambda j, i: (i, 0)))
        out_shape.append(jax.ShapeDtypeStruct((M, N), BF16))
    outs = pl.pallas_call(
        functools.partial(_mm_kernel, pre=pre, post=post, nxt=nxt, two=two),
        grid=grid, in_specs=in_specs, out_specs=out_specs, out_shape=out_shape,
        compiler_params=_cparams("parallel", "parallel"),
    )(*args)
    return outs if nxt else outs[0]


FFN_ROW_CHUNK = 16
NORM_ROWS = 256
HIST = SUBLANES
DOWN_PANEL = 512
FFN_STAGES = 2


def _ffn_kernel(*refs, seg, tiles_per_seq, nxt):
    (x_ref, gpre_ref, wg_ref, wu_ref, cwg_ref, cwu_ref, cbg_ref, cbu_ref, wd_ref,
     bufg_ref, bufu_ref, gpost_ref) = refs[:12]
    rest = list(refs[12:])
    gnext_ref = rest.pop(0) if nxt else None
    y_ref, sg_ref, su_ref = rest[:3]
    rest = rest[3:]
    h_ref = rest.pop(0) if nxt else None
    xn_ref, u_ref, hm_ref, carry_ref = rest
    i, j = pl.program_id(0), pl.program_id(1)
    nf = pl.num_programs(1)
    tm = x_ref.shape[0]
    tf = wg_ref.shape[1]
    R = FFN_ROW_CHUNK

    norm_rows = [slice(r0, min(r0 + NORM_ROWS, tm)) for r0 in range(0, tm, NORM_ROWS)]

    @pl.when(j == 0)
    def _():
        for rows in norm_rows:
            xn_ref[rows, :] = _rms(x_ref[rows, :], gpre_ref[...]).astype(BF16)
            y_ref[rows, :] = jnp.zeros_like(y_ref[rows, :])

    halves = ((wg_ref, cwg_ref, cbg_ref, bufg_ref, sg_ref), (wu_ref, cwu_ref, cbu_ref, bufu_ref, su_ref))
    rps = tm // FFN_STAGES

    if seg is None:
        for hidx, (_, _, _, buf_ref, _) in enumerate(halves):
            @pl.when((i % tiles_per_seq) == 0)
            def _(buf_ref=buf_ref, hidx=hidx):
                u_ref[hidx, 0:HIST, :] = buf_ref[0]

            @pl.when((i % tiles_per_seq) != 0)
            def _(hidx=hidx):
                u_ref[hidx, 0:HIST, :] = carry_ref[j, hidx]

    def up(a):
        for hidx, (w_ref, _, _, _, _) in enumerate(halves):
            u_ref[hidx, HIST + a:HIST + a + rps, :] = _dot(xn_ref[a:a + rps, :], w_ref[...])

    def gate(a):
        for r0 in range(a, a + rps, R):
            conv = []
            for hidx, (_, cw_ref, cb_ref, buf_ref, _) in enumerate(halves):
                if seg is None:
                    ext = u_ref[hidx, r0:r0 + R + HIST, :]
                    cur = ext[HIST:]
                    s1 = pltpu.roll(ext, 1, axis=0)[HIST:]
                    s2 = pltpu.roll(ext, 2, axis=0)[HIST:]
                else:
                    nsq = R // seg
                    cur = u_ref[hidx, r0 + HIST:r0 + HIST + R, :]
                    b = buf_ref[r0 // seg:r0 // seg + nsq]
                    b0 = jnp.broadcast_to(b[:, 0:1, :], (nsq, seg, tf)).reshape(R, tf)
                    b1 = jnp.broadcast_to(b[:, 1:2, :], (nsq, seg, tf)).reshape(R, tf)
                    t = lax.broadcasted_iota(jnp.int32, (R, 1), 0) & (seg - 1)
                    s1 = jnp.where(t == 0, b1, pltpu.roll(cur, 1, axis=0))
                    s2 = jnp.where(t == 0, b0, jnp.where(t == 1, b1, pltpu.roll(cur, 2, axis=0)))
                conv.append(cw_ref[0:1] * s2 + cw_ref[1:2] * s1 + cw_ref[2:3] * cur + cb_ref[...])
            gate_c, up_c = conv
            hm_ref[r0:r0 + R, :] = (gate_c * jax.nn.sigmoid(gate_c) * up_c).astype(BF16)

    def down(a):
        for n0 in range(0, y_ref.shape[1], DOWN_PANEL):
            n1 = min(n0 + DOWN_PANEL, y_ref.shape[1])
            y_ref[a:a + rps, n0:n1] += _dot(hm_ref[a:a + rps, :], wd_ref[:, n0:n1])

    for st in range(FFN_STAGES):
        up(st * rps)
    for hidx, (_, _, _, _, s_ref) in enumerate(halves):
        if seg is None:
            carry_ref[j, hidx] = u_ref[hidx, tm:tm + HIST, :]
            s_ref[0] = u_ref[hidx, tm:tm + HIST, :]
        else:
            s_ref[...] = u_ref[hidx, HIST:HIST + tm, :].reshape(tm // seg, seg, tf)[:, seg - 2:seg, :]
    for st in range(FFN_STAGES):
        gate(st * rps)
        down(st * rps)

    @pl.when(j == nf - 1)
    def _():
        for rows in norm_rows:
            out = x_ref[rows, :] + _rms(y_ref[rows, :], gpost_ref[...])
            y_ref[rows, :] = out
            if nxt:
                h_ref[rows, :] = _rms(out, gnext_ref[...]).astype(BF16)


def conv_ffn(x, g_pre, g_post, w_up, conv_w, conv_b, w_down, buf, *, layer, seq_len, buf_block0=0, g_next=None,
             tm=1024, tf=512):
    M, D = x.shape
    F = w_down.shape[1]
    nseq = M // seq_len
    tf = _tile(F, tf, LANES)
    nf = F // tf
    nxt = g_next is not None
    if seq_len >= LANES:
        tm = _tile(seq_len, tm, BF16_ROWS)
        seg, tiles_per_seq = None, seq_len // tm
        buf = jnp.pad(buf, ((0, 0), (HIST - 2, 0), (0, 0)))
        buf_spec = lambda half: pl.BlockSpec((1, HIST, tf), lambda i, j: (buf_block0 * nseq + i // tiles_per_seq, 0, half * nf + j))
        st_spec = pl.BlockSpec((1, HIST, tf), lambda i, j: (i, 0, j))
        st_shape = jax.ShapeDtypeStruct((M // tm, HIST, F), F32)
    else:
        tm = _tile(M, tm, max(seq_len, FFN_ROW_CHUNK))
        seg, tiles_per_seq = seq_len, 1
        spt = tm // seq_len
        assert seq_len & (seq_len - 1) == 0 and FFN_ROW_CHUNK % seq_len == 0
        buf_spec = lambda half: pl.BlockSpec((spt, 2, tf), lambda i, j: (buf_block0 * (nseq // spt) + i, 0, half * nf + j))
        st_spec = pl.BlockSpec((spt, 2, tf), lambda i, j: (i, 0, j))
        st_shape = jax.ShapeDtypeStruct((nseq, 2, F), F32)
    assert tm % (FFN_STAGES * FFN_ROW_CHUNK) == 0
    row = lambda a: a.reshape(1, -1).astype(F32)
    vec = pl.BlockSpec((1, D), lambda i, j: (0, 0))
    once = dict(pipeline_mode=pl.Buffered(1))
    in_specs = [
        pl.BlockSpec((tm, D), lambda i, j: (i, 0), **once),
        vec,
        pl.BlockSpec((None, D, tf), lambda i, j: (layer, 0, j)),
        pl.BlockSpec((None, D, tf), lambda i, j: (layer, 0, nf + j)),
        pl.BlockSpec((3, tf), lambda i, j: (0, j)),
        pl.BlockSpec((3, tf), lambda i, j: (0, nf + j)),
        pl.BlockSpec((1, tf), lambda i, j: (0, j)),
        pl.BlockSpec((1, tf), lambda i, j: (0, nf + j)),
        pl.BlockSpec((None, tf, D), lambda i, j: (layer, j, 0)),
        buf_spec(0), buf_spec(1), vec,
    ]
    args = [x, row(g_pre), w_up, w_up, conv_w, conv_w, row(conv_b), row(conv_b), w_down, buf, buf, row(g_post)]
    out_specs = [pl.BlockSpec((tm, D), lambda i, j: (i, 0), **once), st_spec, st_spec]
    out_shape = [jax.ShapeDtypeStruct((M, D), F32), st_shape, st_shape]
    if nxt:
        in_specs.append(vec)
        args.append(row(g_next))
        out_specs.append(pl.BlockSpec((tm, D), lambda i, j: (i, 0), **once))
        out_shape.append(jax.ShapeDtypeStruct((M, D), BF16))
    outs = pl.pallas_call(
        functools.partial(_ffn_kernel, seg=seg, tiles_per_seq=tiles_per_seq, nxt=nxt),
        grid=(M // tm, nf), in_specs=in_specs, out_specs=out_specs, out_shape=out_shape,
        scratch_shapes=[pltpu.VMEM((tm, D), BF16), pltpu.VMEM((2, HIST + tm, tf), F32),
                        pltpu.VMEM((tm, tf), BF16), pltpu.VMEM((nf, 2, HIST, tf), F32)],
        compiler_params=_cparams("arbitrary", "arbitrary"),
    )(*args)
    y, sg, su = outs[:3]
    state = jnp.concatenate([sg, su], axis=-1)
    if seg is None:
        state = state[tiles_per_seq - 1::tiles_per_seq, HIST - 2:]
    return y, state, (outs[3] if nxt else None)


def _rope(x, cos, sin_signed):
    W = x.shape[-1]
    lane = lax.broadcasted_iota(jnp.int32, (1, W), 1)
    swapped = jnp.where((lane & 63) < 32, pltpu.roll(x, W - 32, axis=1), pltpu.roll(x, 32, axis=1))
    return x * cos + swapped * sin_signed


def rope_tables(pos, width, rows):
    inv = ROPE_THETA ** (-jnp.arange(0, 64, 2, dtype=F32) / 64)
    ang = pos.astype(F32)[:, None] * inv[None, :]
    c, s = jnp.cos(ang), jnp.sin(ang)
    cos = jnp.tile(jnp.concatenate([c, c], axis=-1), (rows // pos.shape[0], width // 64))
    sin = jnp.tile(jnp.concatenate([-s, s], axis=-1), (rows // pos.shape[0], width // 64))
    return cos, sin


def _mla_pre_kernel(q_ref, ckv_ref, kpe_ref, gkv_ref, wuk_ref, cos_ref, sin_ref,
                    qcat_ref, kcat_ref, ckvn_ref, kper_ref, *, heads, nope, lora):
    cos, sin = cos_ref[...], sin_ref[...]
    q = q_ref[...]
    qpe = _rope(q[:, heads * nope:], cos, sin)
    for h in range(heads):
        qn = q[:, h * nope:(h + 1) * nope].astype(BF16)
        qcat_ref[h, :, 0:lora] = _dot(qn, wuk_ref[h]).astype(BF16)
        qcat_ref[h, :, lora:lora + LANES] = qpe[:, h * LANES:(h + 1) * LANES].astype(BF16)
    ckvn = _rms(ckv_ref[...], gkv_ref[...])
    kper = _rope(kpe_ref[...], cos[:, :LANES], sin[:, :LANES])
    ckvn_ref[...] = ckvn
    kper_ref[...] = kper
    kcat_ref[:, 0:lora] = ckvn.astype(BF16)
    kcat_ref[:, lora:lora + LANES] = kper.astype(BF16)


def mla_pre(q, z, g_kv, w_ukT, cos, sin, *, ckv_col, kpe_col, tm=256):
    M = q.shape[0]
    heads, nope, lora = w_ukT.shape
    tm = _tile(M, min(tm, cos.shape[0]), BF16_ROWS)
    ntab = cos.shape[0] // tm
    wq = q.shape[1]
    kw = lora + LANES
    return pl.pallas_call(
        functools.partial(_mla_pre_kernel, heads=heads, nope=nope, lora=lora),
        grid=(M // tm,),
        in_specs=[
            pl.BlockSpec((tm, wq), lambda i: (i, 0)),
            pl.BlockSpec((tm, lora), lambda i: (i, ckv_col)),
            pl.BlockSpec((tm, LANES), lambda i: (i, kpe_col)),
            pl.BlockSpec((1, lora), lambda i: (0, 0)),
            pl.BlockSpec((heads, nope, lora), lambda i: (0, 0, 0)),
            pl.BlockSpec((tm, heads * LANES), lambda i: (i % ntab, 0)),
            pl.BlockSpec((tm, heads * LANES), lambda i: (i % ntab, 0)),
        ],
        out_specs=[
            pl.BlockSpec((heads, tm, kw), lambda i: (0, i, 0)),
            pl.BlockSpec((tm, kw), lambda i: (i, 0)),
            pl.BlockSpec((tm, lora), lambda i: (i, 0)),
            pl.BlockSpec((tm, LANES), lambda i: (i, 0)),
        ],
        out_shape=[jax.ShapeDtypeStruct((heads, M, kw), BF16),
                   jax.ShapeDtypeStruct((M, kw), BF16),
                   jax.ShapeDtypeStruct((M, lora), F32),
                   jax.ShapeDtypeStruct((M, LANES), F32)],
        compiler_params=_cparams("parallel"),
    )(q, z, z, g_kv.reshape(1, lora).astype(F32), w_ukT, cos, sin)


def _mla_prompt_kernel(q_ref, k_ref, wuv_ref, y_ref, m_ref, l_ref, acc_ref, *, scale, lora, vdim):
    qi, ki = pl.program_id(1), pl.program_id(2)
    heads, tq, kw = q_ref.shape
    tk = k_ref.shape[0]

    @pl.when(ki == 0)
    def _():
        m_ref[...] = jnp.full_like(m_ref, NEG_INF)
        l_ref[...] = jnp.zeros_like(l_ref)
        acc_ref[...] = jnp.zeros_like(acc_ref)

    def step(masked):
        q = q_ref[...].reshape(heads * tq, kw)
        k = k_ref[...]
        s = _dot_nt(q, k) * scale
        if masked:
            qpos = qi * tq + (lax.broadcasted_iota(jnp.int32, (heads * tq, 1), 0) & (tq - 1))
            kpos = ki * tk + lax.broadcasted_iota(jnp.int32, (1, tk), 1)
            s = jnp.where(kpos <= qpos, s, NEG_INF)
        m_old = m_ref[...]
        m_new = jnp.maximum(m_old, jnp.max(s, axis=-1, keepdims=True))
        alpha = jnp.exp(m_old - m_new)
        p = jnp.exp(s - _lane_tile(m_new, tk // LANES))
        l_ref[...] = alpha * l_ref[...] + jnp.sum(p, axis=-1, keepdims=True)
        acc_ref[...] = _lane_tile(alpha, lora // LANES) * acc_ref[...] + _dot(p.astype(BF16), k[:, :lora])
        m_ref[...] = m_new

    needed = ki * tk <= qi * tq + tq - 1
    unmasked = ki * tk + tk - 1 <= qi * tq

    @pl.when(needed & unmasked)
    def _():
        step(False)

    @pl.when(needed & jnp.logical_not(unmasked))
    def _():
        step(True)

    @pl.when(ki == pl.num_programs(2) - 1)
    def _():
        o = (acc_ref[...] / _lane_tile(l_ref[...], lora // LANES)).astype(BF16)
        for h in range(heads):
            y_ref[:, h * vdim:(h + 1) * vdim] = _dot(o[h * tq:(h + 1) * tq], wuv_ref[:, h * vdim:(h + 1) * vdim])


def mla_attn_prompt(qcat, kcat, w_uv, *, batch, scale, tq=128, tk=512):
    heads, M, kw = qcat.shape
    S = M // batch
    lora, hv = w_uv.shape
    vdim = hv // heads
    tq = _tile(S, tq, BF16_ROWS)
    tk = _tile(S, tk, LANES)
    assert tq & (tq - 1) == 0 and tk % LANES == 0 and lora % LANES == 0
    nq, nk = S // tq, S // tk

    def kmap(b, qi, ki):
        return (b * nk + jnp.minimum(ki, (qi * tq + tq - 1) // tk), 0)

    return pl.pallas_call(
        functools.partial(_mla_prompt_kernel, scale=scale, lora=lora, vdim=vdim),
        grid=(batch, nq, nk),
        in_specs=[
            pl.BlockSpec((heads, tq, kw), lambda b, qi, ki: (0, b * nq + qi, 0)),
            pl.BlockSpec((tk, kw), kmap),
            pl.BlockSpec((lora, hv), lambda b, qi, ki: (0, 0)),
        ],
        out_specs=pl.BlockSpec((tq, hv), lambda b, qi, ki: (b * nq + qi, 0)),
        out_shape=jax.ShapeDtypeStruct((M, hv), F32),
        scratch_shapes=[pltpu.VMEM((heads * tq, LANES), F32), pltpu.VMEM((heads * tq, LANES), F32),
                        pltpu.VMEM((heads * tq, lora), F32)],
        compiler_params=_cparams("parallel", "parallel", "arbitrary"),
    )(qcat, kcat, w_uv)


def _mla_sample_kernel(pt_ref, q_ref, ckvn_ref, kper_ref, wuv_ref, poolc_ref, poolp_ref, y_ref,
                       kbuf_ref, pbuf_ref, knew_ref, sem_ref,
                       *, layer, scale, lora, rope, vdim, seq, n_pages, page, chunk):
    b = pl.program_id(0)
    slot = b % 2
    rows = q_ref.shape[1]

    def page_copies(seq_idx, sl, p):
        src_page = pt_ref[seq_idx, p]
        dst = pl.ds(pl.multiple_of(p * page, page), page)
        return (pltpu.make_async_copy(poolc_ref.at[layer, src_page], kbuf_ref.at[sl, dst], sem_ref.at[0, sl]),
                pltpu.make_async_copy(poolp_ref.at[layer, src_page], pbuf_ref.at[sl, p], sem_ref.at[1, sl]))

    def for_pages(seq_idx, sl, action):
        def body(p, carry):
            for cp in page_copies(seq_idx, sl, p):
                action(cp)
            return carry
        lax.fori_loop(0, n_pages, body, 0)

    @pl.when(b == 0)
    def _():
        for_pages(0, 0, lambda cp: cp.start())

    @pl.when(b + 1 < pl.num_programs(0))
    def _():
        for_pages(b + 1, 1 - slot, lambda cp: cp.start())

    for_pages(b, slot, lambda cp: cp.wait())

    q = q_ref[0]
    q_lat, q_pe = q[:, :lora], q[:, lora:lora + rope]
    scores, vals = [], []
    for c0 in range(0, n_pages * page, chunk):
        kc = kbuf_ref[slot, c0:c0 + chunk, :].astype(BF16)
        kp_t = jnp.concatenate([pbuf_ref[slot, p] for p in range(c0 // page, (c0 + chunk) // page)], axis=1)
        scores.append((_dot_nt(q_lat, kc) + _dot(q_pe, kp_t.astype(BF16))) * scale)
        vals.append(kc)
    knew_ref[...] = jnp.zeros_like(knew_ref)
    knew_ref[0:seq, 0:lora] = ckvn_ref[...]
    knew_ref[0:seq, lora:lora + LANES] = kper_ref[...]
    kn = knew_ref[...].astype(BF16)
    qidx = lax.broadcasted_iota(jnp.int32, (rows, 1), 0) & (seq - 1)
    kidx = lax.broadcasted_iota(jnp.int32, (1, page), 1)
    scores.append(jnp.where(kidx <= qidx, _dot_nt(q, kn) * scale, NEG_INF))
    vals.append(kn[:, :lora])
    m = functools.reduce(jnp.maximum, [jnp.max(s, axis=-1, keepdims=True) for s in scores])
    probs = [jnp.exp(s - m) for s in scores]
    l = functools.reduce(jnp.add, [jnp.sum(p, axis=-1, keepdims=True) for p in probs])
    acc = functools.reduce(jnp.add, [_dot(p.astype(BF16), v) for p, v in zip(probs, vals)])
    o = (acc / l).astype(BF16)
    full = _dot(o, wuv_ref[...])
    for h in range(rows // seq):
        y_ref[:, h * vdim:(h + 1) * vdim] = full[h * seq:(h + 1) * seq, h * vdim:(h + 1) * vdim]


def mla_attn_sample(qs, pool_ckv, pool_kpe_t, layer, page_table, ckv_n, kpe_r, w_uv, *, scale, chunk_pages=16):
    B, rows, kw = qs.shape
    _, _, page, lora = pool_ckv.shape
    rope = pool_kpe_t.shape[2]
    n_pages = page_table.shape[1]
    chunk = _tile(n_pages, chunk_pages, 1) * page
    hv = w_uv.shape[1]
    seq = ckv_n.shape[0] // B
    vdim = hv // (rows // seq)
    assert seq & (seq - 1) == 0 and seq <= page
    return pl.pallas_call(
        functools.partial(_mla_sample_kernel, layer=layer, scale=scale, lora=lora, rope=rope, vdim=vdim, seq=seq,
                          n_pages=n_pages, page=page, chunk=chunk),
        grid_spec=pltpu.PrefetchScalarGridSpec(
            num_scalar_prefetch=1, grid=(B,),
            in_specs=[pl.BlockSpec((1, rows, kw), lambda b, pt: (b, 0, 0)),
                      pl.BlockSpec((seq, lora), lambda b, pt: (b, 0)),
                      pl.BlockSpec((seq, LANES), lambda b, pt: (b, 0)),
                      pl.BlockSpec((lora, hv), lambda b, pt: (0, 0)),
                      pl.BlockSpec(memory_space=pl.ANY),
                      pl.BlockSpec(memory_space=pl.ANY)],
            out_specs=pl.BlockSpec((seq, hv), lambda b, pt: (b, 0)),
            scratch_shapes=[pltpu.VMEM((2, n_pages * page, lora), F32),
                            pltpu.VMEM((2, n_pages, rope, page), F32),
                            pltpu.VMEM((page, kw), F32),
                            pltpu.SemaphoreType.DMA((2, 2))]),
        out_shape=jax.ShapeDtypeStruct((B * seq, hv), F32),
        compiler_params=_cparams("arbitrary"),
    )(page_table, qs, ckv_n, kpe_r, w_uv, pool_ckv, pool_kpe_t)


def _dwconv_kernel(*refs, mode, taps, hist):
    if mode == "conf":
        a_ref, b_ref, st_ref, w_ref, cb_ref, lg_ref, lb_ref, y_ref, so_ref, buf_ref, c_ref = refs
    else:
        gate_ref, a_ref, b_ref, st_ref, w_ref, y_ref, so_ref, buf_ref, c_ref = refs
    t = pl.program_id(1)
    tt, C = a_ref.shape

    @pl.when(t == 0)
    def _():
        buf_ref[0:hist] = st_ref[0]

    if mode == "conf":
        u = a_ref[...] * jax.nn.sigmoid(b_ref[...])
    else:
        u = a_ref[...] * b_ref[...]
    buf_ref[hist:hist + tt] = u
    rc = min(tt, 64)
    for c0 in range(0, C, LANES):
        for r0 in range(0, tt, rc):
            acc = None
            for k in range(taps):
                term = (w_ref[k:k + 1, c0:c0 + LANES]
                        * buf_ref[r0 + hist - (taps - 1) + k:r0 + hist - (taps - 1) + k + rc, c0:c0 + LANES])
                acc = term if acc is None else acc + term
            c_ref[r0:r0 + rc, c0:c0 + LANES] = acc
    c = c_ref[...]
    if mode == "conf":
        c = c + cb_ref[...]
        mu = jnp.mean(c, axis=-1, keepdims=True)
        d = c - mu
        var = jnp.mean(d * d, axis=-1, keepdims=True)
        yn = d * lax.rsqrt(var + EPS) * lg_ref[...] + lb_ref[...]
        y_ref[...] = yn * jax.nn.sigmoid(yn)
    else:
        y_ref[...] = gate_ref[...] * c
    so_ref[0] = buf_ref[tt:tt + hist]
    if tt >= hist:
        buf_ref[0:hist] = buf_ref[tt:tt + hist]


def dwconv(z, state, w, *, mode, cols, batch, extra=(), tt=128):
    M = z.shape[0]
    taps, C = w.shape
    S = M // batch
    hist = -(-(taps - 1) // SUBLANES) * SUBLANES
    tt = _tile(S, tt)
    nt = S // tt
    assert nt == 1 or tt >= hist
    st = jnp.pad(state.astype(F32), ((0, 0), (hist - (taps - 1), 0), (0, 0)))
    colspec = lambda c: pl.BlockSpec((tt, C), lambda b, t: (b * nt + t, c))
    vec = pl.BlockSpec((1, C), lambda b, t: (0, 0))
    in_specs = [colspec(c) for c in cols]
    in_specs += [pl.BlockSpec((1, hist, C), lambda b, t: (b, 0, 0)), pl.BlockSpec((taps, C), lambda b, t: (0, 0))]
    in_specs += [vec] * len(extra)
    y, so = pl.pallas_call(
        functools.partial(_dwconv_kernel, mode=mode, taps=taps, hist=hist),
        grid=(batch, nt),
        in_specs=in_specs,
        out_specs=[pl.BlockSpec((tt, C), lambda b, t: (b * nt + t, 0)),
                   pl.BlockSpec((1, hist, C), lambda b, t: (b, 0, 0))],
        out_shape=[jax.ShapeDtypeStruct((M, C), F32), jax.ShapeDtypeStruct((batch, hist, C), F32)],
        scratch_shapes=[pltpu.VMEM((hist + tt, C), F32), pltpu.VMEM((tt, C), F32)],
        compiler_params=_cparams("parallel", "arbitrary"),
    )(*([z] * len(cols)), st, w.astype(F32), *[e.reshape(1, C).astype(F32) for e in extra])
    return y, so[:, hist - (taps - 1):]


def _swa_rope_kernel(q_ref, k_ref, cos_ref, sin_ref, qo_ref, ko_ref):
    cos, sin = cos_ref[...], sin_ref[...]
    kw = k_ref.shape[1]
    qo_ref[...] = _rope(q_ref[...], cos, sin)
    ko_ref[...] = _rope(k_ref[...], cos[:, :kw], sin[:, :kw])


def swa_rope(z, cos, sin, *, q_col, k_col, qw, kw, tm=256):
    M = z.shape[0]
    tm = _tile(M, min(tm, cos.shape[0]))
    ntab = cos.shape[0] // tm
    return pl.pallas_call(
        _swa_rope_kernel,
        grid=(M // tm,),
        in_specs=[pl.BlockSpec((tm, qw), lambda i: (i, q_col)),
                  pl.BlockSpec((tm, kw), lambda i: (i, k_col)),
                  pl.BlockSpec((tm, qw), lambda i: (i % ntab, 0)),
                  pl.BlockSpec((tm, qw), lambda i: (i % ntab, 0))],
        out_specs=[pl.BlockSpec((tm, qw), lambda i: (i, 0)), pl.BlockSpec((tm, kw), lambda i: (i, 0))],
        out_shape=[jax.ShapeDtypeStruct((M, qw), F32), jax.ShapeDtypeStruct((M, kw), F32)],
        compiler_params=_cparams("parallel"),
    )(z, z, cos, sin)


def _sink_softmax_pv(s, sink, v):
    m = jnp.maximum(jnp.max(s, axis=-1, keepdims=True), sink)
    p = jnp.exp(s - m)
    denom = jnp.sum(p, axis=-1, keepdims=True) + jnp.exp(sink - m)
    return _dot(p.astype(BF16), v) / denom


def _swa_prompt_kernel(sink_ref, q_ref, kc_ref, kp_ref, vc_ref, vp_ref, y_ref, *, scale, hd, group):
    n = pl.program_id(1)
    W = q_ref.shape[0]
    kvh = kc_ref.shape[1] // hd
    qi = lax.broadcasted_iota(jnp.int32, (W, 1), 0)
    si = lax.broadcasted_iota(jnp.int32, (1, 2 * W), 1)
    rel = qi + W - si
    valid = (rel >= 0) & (rel < W) & (n * W + si - W >= 0)
    q = q_ref[...].astype(BF16)
    kk = jnp.concatenate([kp_ref[...], kc_ref[...]], axis=0).astype(BF16)
    vv = jnp.concatenate([vp_ref[...], vc_ref[...]], axis=0).astype(BF16)
    outs = []
    for kv in range(kvh):
        k_h = kk[:, kv * hd:(kv + 1) * hd]
        v_h = vv[:, kv * hd:(kv + 1) * hd]
        for g in range(group):
            h = kv * group + g
            s = _dot_nt(q[:, h * hd:(h + 1) * hd], k_h) * scale
            s = jnp.where(valid, s, NEG_INF)
            outs.append(_sink_softmax_pv(s, sink_ref[h], v_h))
    y_ref[...] = jnp.concatenate(outs, axis=-1)


def swa_prompt(q, k, z, sinks, *, v_col, batch, scale, hd):
    M, qw = q.shape
    kw = k.shape[1]
    S = M // batch
    W = WINDOW
    nb = S // W
    group = qw // kw
    cur = lambda b, n: b * nb + n
    prev = lambda b, n: b * nb + jnp.maximum(n - 1, 0)
    return pl.pallas_call(
        functools.partial(_swa_prompt_kernel, scale=scale, hd=hd, group=group),
        grid=(batch, nb),
        in_specs=[pl.BlockSpec(memory_space=pltpu.SMEM),
                  pl.BlockSpec((W, qw), lambda b, n: (cur(b, n), 0)),
                  pl.BlockSpec((W, kw), lambda b, n: (cur(b, n), 0)),
                  pl.BlockSpec((W, kw), lambda b, n: (prev(b, n), 0)),
                  pl.BlockSpec((W, kw), lambda b, n: (cur(b, n), v_col)),
                  pl.BlockSpec((W, kw), lambda b, n: (prev(b, n), v_col))],
        out_specs=pl.BlockSpec((W, qw), lambda b, n: (cur(b, n), 0)),
        out_shape=jax.ShapeDtypeStruct((M, qw), F32),
        compiler_params=_cparams("parallel", "parallel"),
    )(sinks.astype(F32), q, k, k, z, z)


def _swa_sample_kernel(sink_ref, q_ref, kb_ref, vb_ref, kn_ref, vn_ref, y_ref, ko_ref, vo_ref,
                       ks_ref, vs_ref, *, scale, hd, group, first_pos, seq):
    nseq, nbuf = kb_ref.shape[0], kb_ref.shape[1]
    npad = ks_ref.shape[0] // nseq
    kvh = kn_ref.shape[1] // hd
    for n in range(nseq):
        rows = slice(n * seq, (n + 1) * seq)
        base = n * npad
        for s_ref, b_ref, n_ref, o_ref in ((ks_ref, kb_ref, kn_ref, ko_ref), (vs_ref, vb_ref, vn_ref, vo_ref)):
            s_ref[base:base + nbuf] = b_ref[n]
            s_ref[base + nbuf:base + nbuf + seq] = n_ref[rows, :]
            s_ref[base + nbuf + seq:base + npad] = jnp.zeros((npad - nbuf - seq, s_ref.shape[1]), F32)
            o_ref[n] = s_ref[base + seq:base + seq + nbuf]
    kk = ks_ref[...].astype(BF16)
    vv = vs_ref[...].astype(BF16)
    r = lax.broadcasted_iota(jnp.int32, (group * nseq * seq, 1), 0)
    c = lax.broadcasted_iota(jnp.int32, (1, nseq * npad), 1)
    q_seq, q_i = _div_pow2(r, seq) & (nseq - 1), r & (seq - 1)
    k_seq, k_j = _div_pow2(c, npad), c & (npad - 1)
    rel = q_i + nbuf - k_j
    valid = (q_seq == k_seq) & (rel >= 0) & (rel < WINDOW) & (first_pos + k_j >= 0)
    q = q_ref[...]
    outs = []
    for kv in range(kvh):
        heads = range(kv * group, (kv + 1) * group)
        q4 = jnp.concatenate([q[:, h * hd:(h + 1) * hd] for h in heads], axis=0)
        sink = jnp.concatenate([jnp.full((nseq * seq, 1), sink_ref[h], F32) for h in heads], axis=0)
        s = _dot_nt(q4.astype(BF16), kk[:, kv * hd:(kv + 1) * hd]) * scale
        o = _sink_softmax_pv(jnp.where(valid, s, NEG_INF), sink, vv[:, kv * hd:(kv + 1) * hd])
        outs += [o[g * nseq * seq:(g + 1) * nseq * seq] for g in range(group)]
    y_ref[...] = jnp.concatenate(outs, axis=-1)


def swa_sample(q, k_new, z, buf_k, buf_v, buf_block0, sinks, *, v_col, batch, scale, hd, past_len, nseq=8):
    M, qw = q.shape
    _, nbuf, kw = buf_k.shape
    seq = M // batch
    group = qw // kw
    npad = 2 * nbuf
    nseq = _tile(batch, nseq, 1)
    steps = batch // nseq
    assert seq & (seq - 1) == 0 and nseq & (nseq - 1) == 0 and nbuf + seq <= npad and nbuf == WINDOW
    row = lambda b: (b, 0)
    bufspec = pl.BlockSpec((nseq, nbuf, kw), lambda b: (buf_block0 * steps + b, 0, 0))
    outbuf = pl.BlockSpec((nseq, nbuf, kw), lambda b: (b, 0, 0))
    return pl.pallas_call(
        functools.partial(_swa_sample_kernel, scale=scale, hd=hd, group=group, first_pos=past_len - nbuf, seq=seq),
        grid=(steps,),
        in_specs=[pl.BlockSpec(memory_space=pltpu.SMEM),
                  pl.BlockSpec((nseq * seq, qw), row),
                  bufspec, bufspec,
                  pl.BlockSpec((nseq * seq, kw), row),
                  pl.BlockSpec((nseq * seq, kw), lambda b: (b, v_col))],
        out_specs=[pl.BlockSpec((nseq * seq, qw), row), outbuf, outbuf],
        out_shape=[jax.ShapeDtypeStruct((M, qw), F32),
                   jax.ShapeDtypeStruct((batch, nbuf, kw), F32),
                   jax.ShapeDtypeStruct((batch, nbuf, kw), F32)],
        scratch_shapes=[pltpu.VMEM((nseq * npad, kw), F32), pltpu.VMEM((nseq * npad, kw), F32)],
        compiler_params=_cparams("parallel"),
    )(sinks.astype(F32), q, buf_k, buf_v, k_new, z)


def _mem_attn_kernel(q_ref, k_ref, v_ref, o_ref, *, scale, hd, nseq, head_rows):
    heads = q_ref.shape[1] // hd
    rows = q_ref.shape[0]
    nk = k_ref.shape[0] // heads if head_rows else k_ref.shape[0]
    q = q_ref[...].astype(BF16)
    if nseq > 1:
        q_seq = _div_pow2(lax.broadcasted_iota(jnp.int32, (rows, 1), 0), rows // nseq)
        k_seq = _div_pow2(lax.broadcasted_iota(jnp.int32, (1, nk), 1), nk // nseq)
        same = q_seq == k_seq
    outs = []
    for h in range(heads):
        sl = slice(h * hd, (h + 1) * hd)
        if head_rows:
            k_h = k_ref[pl.ds(h, nk, stride=heads), :].astype(BF16)
            v_h = v_ref[pl.ds(h, nk, stride=heads), :].astype(BF16)
        else:
            k_h = k_ref[:, sl].astype(BF16)
            v_h = v_ref[:, sl].astype(BF16)
        s = _dot_nt(q[:, sl], k_h) * scale
        if nseq > 1:
            s = jnp.where(same, s, NEG_INF)
        m = jnp.max(s, axis=-1, keepdims=True)
        p = jnp.exp(s - m)
        p = p / jnp.sum(p, axis=-1, keepdims=True)
        outs.append(_dot(p.astype(BF16), v_h))
    o_ref[...] = jnp.concatenate(outs, axis=-1)


def mem_attn(q, k_arr, v_arr, *, k_col=0, v_col=0, kv_block0=0, head_rows=False, batch, mem_tokens, scale, hd,
             tq=256, nseq=8):
    M, w = q.shape
    heads = w // hd
    S = M // batch
    tq = _tile(S, tq)
    nq = S // tq
    nseq = _tile(batch, nseq, 1) if nq == 1 else 1
    nb = batch // nseq
    if head_rows:
        kv_spec = lambda col: pl.BlockSpec((nseq * mem_tokens * heads, hd), lambda b, i: (kv_block0 * nb + b, 0))
    else:
        kv_spec = lambda col: pl.BlockSpec((nseq * mem_tokens, w), lambda b, i: (kv_block0 * nb + b, col))
    return pl.pallas_call(
        functools.partial(_mem_attn_kernel, scale=scale, hd=hd, nseq=nseq, head_rows=head_rows),
        grid=(nb, nq),
        in_specs=[pl.BlockSpec((nseq * tq, w), lambda b, i: (b * nq + i, 0)), kv_spec(k_col), kv_spec(v_col)],
        out_specs=pl.BlockSpec((nseq * tq, w), lambda b, i: (b * nq + i, 0)),
        out_shape=jax.ShapeDtypeStruct((M, w), F32),
        compiler_params=_cparams("parallel", "parallel"),
    )(q, k_arr, v_arr)


def _prep_weights(p):
    dims = p["dims"]
    H, nope, rope, lora, qlora = dims["mla_heads"], dims["nope"], dims["rope"], dims["lora"], dims["qlora"]
    w = {}
    wi = p["w_even_in"]
    n_in = wi.shape[-1]
    pad = jnp.zeros(wi.shape[:2] + (-(-(n_in + LANES - rope) // 512) * 512 - n_in,), wi.dtype)
    w["even_in"] = jnp.concatenate(
        [wi[..., qlora + lora + rope:], wi[..., :qlora + lora + rope], pad], axis=-1).astype(BF16)
    wq = p["w_q_b"].reshape(p["w_q_b"].shape[0], qlora, H, nope + rope)
    wq_pe = jnp.pad(wq[..., nope:], ((0, 0), (0, 0), (0, 0), (0, LANES - rope)))
    w["q_b"] = jnp.concatenate([wq[..., :nope].reshape(-1, qlora, H * nope),
                                wq_pe.reshape(-1, qlora, H * LANES)], axis=-1).astype(BF16)
    w["ukT"] = jnp.transpose(p["w_uk"], (0, 2, 3, 1)).astype(BF16)
    w["uv"] = p["w_uv"].reshape(p["w_uv"].shape[0], lora, -1).astype(BF16)
    for name in ("w_even_out", "w_odd_in", "w_odd_out", "w_mem_q", "w_mem_kv", "w_mem_o", "w_ffn_up", "w_ffn_down"):
        w[name] = p[name].astype(BF16)
    return w


def _trunk(x, pos, p, w, mem, past, page_table):
    dims = p["dims"]
    B, S, D = x.shape
    M = B * S
    H, nope, rope, lora, qlora = dims["mla_heads"], dims["nope"], dims["rope"], dims["lora"], dims["qlora"]
    conf_ch, sc_ch = dims["conf_ch"], dims["sc_ch"]
    swa_h, swa_kv, swa_hd = dims["swa_heads"], dims["swa_kv"], dims["swa_hd"]
    mem_h, mem_hd = dims["mem_heads"], dims["mem_hd"]
    mem_w = mem_h * mem_hd
    ff = dims["d_ff"]
    depth = p["norm_gains"].shape[0]
    prompt = past is None
    x = x.reshape(M, D)
    tab_rows = S if prompt else max(S, min(M, 256))
    cos, sin = rope_tables(pos, max(H * LANES, swa_h * swa_hd), tab_rows)
    new = {k: [] for k in ("mla_ckv", "mla_kpe", "swa_k", "swa_v", "mem_k", "mem_v", "conf_conv", "sc_conv", "ffn_conv")}
    if not prompt:
        mem_k_all = past["mem_k"].reshape(-1, mem_hd)
        mem_v_all = past["mem_v"].reshape(-1, mem_hd)
        kpe_pool_t = jnp.swapaxes(past["mla_kpe"], 2, 3)
        swa_k_all = past["swa_k"].reshape((-1,) + past["swa_k"].shape[2:3] + (swa_kv * swa_hd,))
        swa_v_all = past["swa_v"].reshape(swa_k_all.shape)
        ffn_all = past["ffn_conv"].reshape((-1,) + past["ffn_conv"].shape[2:])
    h = None
    for layer in range(depth):
        g = p["norm_gains"][layer]
        w_in = w["even_in"][layer // 2] if layer % 2 == 0 else w["w_odd_in"][layer // 2]
        z = matmul_fused(x, w_in, g_pre=g[0]) if h is None else matmul_fused(h, w_in, tm=512)
        if layer % 2 == 0:
            i = layer // 2
            glu_w = 2 * conf_ch
            q = matmul_fused(z, w["q_b"][i], g_pre=p["g_q_a"][i], xcol=glu_w // qlora)
            qcat, kcat, ckv_n, kpe_r = mla_pre(
                q, z, p["g_kv_a"][i], w["ukT"][i], cos[:, :H * LANES], sin[:, :H * LANES],
                ckv_col=(glu_w + qlora) // lora, kpe_col=(glu_w + qlora + lora) // LANES)
            scale = (nope + rope) ** -0.5
            if prompt:
                y_mla = mla_attn_prompt(qcat, kcat, w["uv"][i], batch=B, scale=scale)
                conf_state = jnp.zeros((B, p["conf_dw_w"].shape[1] - 1, conf_ch), F32)
            else:
                qs = qcat.reshape(H, B, S, lora + LANES).transpose(1, 0, 2, 3).reshape(B, H * S, lora + LANES)
                y_mla = mla_attn_sample(qs, past["mla_ckv"], kpe_pool_t, i, page_table,
                                        ckv_n, kpe_r, w["uv"][i], scale=scale)
                conf_state = past["conf_conv"][i]
            c, conf_new = dwconv(z, conf_state, p["conf_dw_w"][i], mode="conf", cols=(0, 1), batch=B,
                                 extra=(p["conf_dw_b"][i], p["conf_ln_g"][i], p["conf_ln_b"][i]))
            x, hq = matmul_fused(y_mla, w["w_even_out"][i], x2=c, g_post=g[1], res=x, g_next=g[2])
            new["mla_ckv"].append(ckv_n.reshape(B, S, lora))
            new["mla_kpe"].append(kpe_r[:, :rope].reshape(B, S, rope))
            new["conf_conv"].append(conf_new)
        else:
            j = layer // 2
            qw, kw = swa_h * swa_hd, swa_kv * swa_hd
            sc_state = jnp.zeros((B, p["sc_conv_w"].shape[1] - 1, sc_ch), F32) if prompt else past["sc_conv"][j]
            y_sc, sc_new = dwconv(z, sc_state, p["sc_conv_w"][j], mode="sc", cols=(0, 1, 2), batch=B)
            q_r, k_r = swa_rope(z, cos[:, :qw], sin[:, :qw], q_col=3 * sc_ch // qw, k_col=(3 * sc_ch + qw) // kw,
                                qw=qw, kw=kw)
            v_col = (3 * sc_ch + qw + kw) // kw
            scale = swa_hd ** -0.5
            if prompt:
                y_att = swa_prompt(q_r, k_r, z, p["swa_sinks"][j], v_col=v_col, batch=B, scale=scale, hd=swa_hd)
                v = z[:, v_col * kw:(v_col + 1) * kw]
                nbuf = min(WINDOW, S)
                nk = k_r.reshape(B, S, swa_kv, swa_hd)[:, S - nbuf:]
                nv = v.reshape(B, S, swa_kv, swa_hd)[:, S - nbuf:]
            else:
                nbuf = swa_k_all.shape[1]
                y_att, nk, nv = swa_sample(q_r, k_r, z, swa_k_all, swa_v_all, j, p["swa_sinks"][j], v_col=v_col,
                                           batch=B, scale=scale, hd=swa_hd, past_len=past["past_len"])
                nk = nk.reshape(B, nbuf, swa_kv, swa_hd)
                nv = nv.reshape(B, nbuf, swa_kv, swa_hd)
            x, hq = matmul_fused(y_sc, w["w_odd_out"][j], x2=y_att, g_post=g[1], res=x, g_next=g[2])
            new["swa_k"].append(nk)
            new["swa_v"].append(nv)
            new["sc_conv"].append(sc_new)
        qm = matmul_fused(hq, w["w_mem_q"][layer], tm=512)
        if prompt:
            mem_tokens = mem.shape[1]
            kv = matmul_fused(mem.reshape(-1, D), w["w_mem_kv"][layer])
            o = mem_attn(qm, kv, kv, k_col=0, v_col=1, kv_block0=0, batch=B, mem_tokens=mem_tokens,
                         scale=mem_hd ** -0.5, hd=mem_hd)
            kv = kv.reshape(B, mem_tokens, 2, mem_h, mem_hd)
            new["mem_k"].append(kv[:, :, 0])
            new["mem_v"].append(kv[:, :, 1])
        else:
            o = mem_attn(qm, mem_k_all, mem_v_all, head_rows=True, kv_block0=layer, batch=B,
                         mem_tokens=past["mem_k"].shape[2], scale=mem_hd ** -0.5, hd=mem_hd)
        x = matmul_fused(o, w["w_mem_o"][layer], g_post=g[3], res=x)
        g_next = p["norm_gains"][layer + 1][0] if layer + 1 < depth else None
        if prompt:
            ffn_buf, block0 = jnp.zeros((B, 2, 2 * ff), F32), 0
        else:
            ffn_buf, block0 = ffn_all, layer
        x, ffn_new, h = conv_ffn(x, g[4], g[5], w["w_ffn_up"], p["ffn_conv_w"][layer], p["ffn_conv_b"][layer],
                                 w["w_ffn_down"], ffn_buf, layer=layer, seq_len=S, buf_block0=block0, g_next=g_next)
        new["ffn_conv"].append(ffn_new)
    return x.reshape(B, S, D), {k: jnp.stack(v) for k, v in new.items() if v}


def kernel(x_prompt, x_sample, cache_mla_ckv, cache_mla_kpe, cache_swa_k, cache_swa_v, cache_mem_k, cache_mem_v,
           state_conf_conv, state_sc_conv, state_ffn_conv, page_table, mem_prompt, norm_gains, w_even_in, g_q_a,
           w_q_b, g_kv_a, w_uk, w_uv, conf_dw_w, conf_dw_b, conf_ln_g, conf_ln_b, w_even_out, w_odd_in, sc_conv_w,
           swa_sinks, w_odd_out, w_mem_q, w_mem_kv, w_mem_o, w_ffn_up, ffn_conv_w, ffn_conv_b, w_ffn_down):
    lora, heads, nope = w_uk.shape[1], w_uk.shape[2], w_uk.shape[3]
    rope = cache_mla_kpe.shape[-1]
    dims = dict(mla_heads=heads, nope=nope, rope=rope, lora=lora, qlora=g_q_a.shape[1],
                conf_ch=conf_dw_w.shape[2], sc_ch=sc_conv_w.shape[2],
                swa_heads=swa_sinks.shape[1], swa_kv=cache_swa_k.shape[3], swa_hd=cache_swa_k.shape[4],
                mem_heads=cache_mem_k.shape[3], mem_hd=cache_mem_k.shape[4], d_ff=w_ffn_down.shape[1])
    assert rope == 64 and dims["swa_hd"] == 64, "rotary helper assumes 64-wide rotary heads"
    p = dict(dims=dims, norm_gains=norm_gains, w_even_in=w_even_in, g_q_a=g_q_a, w_q_b=w_q_b, g_kv_a=g_kv_a,
             w_uk=w_uk, w_uv=w_uv, conf_dw_w=conf_dw_w, conf_dw_b=conf_dw_b, conf_ln_g=conf_ln_g,
             conf_ln_b=conf_ln_b, w_even_out=w_even_out, w_odd_in=w_odd_in, sc_conv_w=sc_conv_w,
             swa_sinks=swa_sinks, w_odd_out=w_odd_out, w_mem_q=w_mem_q, w_mem_kv=w_mem_kv, w_mem_o=w_mem_o,
             w_ffn_up=w_ffn_up, ffn_conv_w=ffn_conv_w, ffn_conv_b=ffn_conv_b, w_ffn_down=w_ffn_down)
    w = _prep_weights(p)
    pos_p = jnp.arange(x_prompt.shape[1], dtype=jnp.int32)
    y_prompt, new_p = _trunk(x_prompt, pos_p, p, w, mem_prompt, None, None)
    past_len = page_table.shape[1] * cache_mla_ckv.shape[2]
    pos_s = past_len + jnp.arange(x_sample.shape[1], dtype=jnp.int32)
    past = dict(mla_ckv=cache_mla_ckv, mla_kpe=cache_mla_kpe, swa_k=cache_swa_k, swa_v=cache_swa_v,
                mem_k=cache_mem_k, mem_v=cache_mem_v, conf_conv=state_conf_conv, sc_conv=state_sc_conv,
                ffn_conv=state_ffn_conv, past_len=past_len)
    y_sample, new_s = _trunk(x_sample, pos_s, p, w, None, past, page_table)
    return (y_prompt, y_sample,
            new_p["mla_ckv"], new_p["mla_kpe"], new_p["swa_k"], new_p["swa_v"], new_p["mem_k"], new_p["mem_v"],
            new_p["conf_conv"], new_p["sc_conv"], new_p["ffn_conv"],
            new_s["mla_ckv"], new_s["mla_kpe"], new_s["swa_k"], new_s["swa_v"],
            new_s["conf_conv"], new_s["sc_conv"], new_s["ffn_conv"])
```

```python
import functools

import jax
import jax.numpy as jnp
from jax import lax
from jax.experimental import pallas as pl
from jax.experimental.pallas import tpu as pltpu

F32 = jnp.float32
BF16 = jnp.bfloat16

EPS = 1e-6
ROPE_THETA = 10000.0
WINDOW = 128
NEG_INF = -1e30

LANES = 128
SUBLANES = 8
BF16_ROWS = 16
VMEM_LIMIT_BYTES = 56 * 1024 * 1024
DMA_QUEUES = 2


def _cparams(*sem):
    return pltpu.CompilerParams(dimension_semantics=sem, vmem_limit_bytes=VMEM_LIMIT_BYTES)


def _tile(n, pref, mult=SUBLANES):
    if n <= pref:
        return n
    t = (pref // mult) * mult
    while t > mult and n % t:
        t -= mult
    assert n % t == 0, (n, pref, mult)
    return t


def _rms(x, g):
    y = x * lax.rsqrt(jnp.mean(x * x, axis=-1, keepdims=True) + EPS)
    return y * g


def _dot(a, b):
    return jnp.dot(a, b, preferred_element_type=F32)


def _dot_nt(a, b):
    return lax.dot_general(a, b, (((1,), (1,)), ((), ())), preferred_element_type=F32)


def _div_pow2(x, n):
    assert n & (n - 1) == 0, n
    return x >> (n.bit_length() - 1)


def _lane_tile(x, n):
    return x if n == 1 else jnp.concatenate([x] * n, axis=1)


def _mm_kernel(*refs, pre, post, nxt, two):
    it = iter(refs)
    x_ref, w_ref = next(it), next(it)
    x2_ref = next(it) if two else None
    gpre_ref = next(it) if pre else None
    gpost_ref, res_ref = (next(it), next(it)) if post else (None, None)
    gnext_ref = next(it) if nxt else None
    o_ref = next(it)
    x = x_ref[...]
    if pre:
        x = _rms(x.astype(F32), gpre_ref[...])
    k1 = x.shape[1]
    acc = _dot(x.astype(BF16), w_ref[0:k1, :])
    if two:
        acc = acc + _dot(x2_ref[...].astype(BF16), w_ref[k1:, :])
    if post:
        acc = res_ref[...] + _rms(acc, gpost_ref[...])
    o_ref[...] = acc.astype(o_ref.dtype)
    if nxt:
        next(it)[...] = _rms(acc, gnext_ref[...]).astype(BF16)


def matmul_fused(x, w, *, x2=None, g_pre=None, g_post=None, res=None, g_next=None, xcol=0, tm=256, tn=1536):
    M = x.shape[0]
    K, N = w.shape
    pre, post, nxt, two = g_pre is not None, g_post is not None, g_next is not None, x2 is not None
    assert post or not nxt
    assert not (two and (pre or xcol))
    tm = _tile(M, tm, BF16_ROWS)
    tn = N if post else _tile(N, tn, LANES)
    grid = (N // tn, M // tm)
    vec = lambda n: pl.BlockSpec((1, n), lambda j, i: (0, 0))
    k1 = K - x2.shape[1] if two else K
    in_specs = [pl.BlockSpec((tm, k1), lambda j, i: (i, xcol)),
                pl.BlockSpec((K, tn), lambda j, i: (0, j))]
    args = [x, w]
    if two:
        in_specs.append(pl.BlockSpec((tm, K - k1), lambda j, i: (i, 0)))
        args.append(x2)
    if pre:
        in_specs.append(vec(K))
        args.append(g_pre.reshape(1, K).astype(F32))
    if post:
        in_specs += [vec(N), pl.BlockSpec((tm, N), lambda j, i: (i, 0))]
        args += [g_post.reshape(1, N).astype(F32), res]
    out_specs = [pl.BlockSpec((tm, tn), lambda j, i: (i, j))]
    out_shape = [jax.ShapeDtypeStruct((M, N), F32)]
    if nxt:
        in_specs.append(vec(N))
        args.append(g_next.reshape(1, N).astype(F32))
        out_specs.append(pl.BlockSpec((tm, N), lambda j, i: (i, 0)))
        out_shape.append(jax.ShapeDtypeStruct((M, N), BF16))
    outs = pl.pallas_call(
        functools.partial(_mm_kernel, pre=pre, post=post, nxt=nxt, two=two),
        grid=grid, in_specs=in_specs, out_specs=out_specs, out_shape=out_shape,
        compiler_params=_cparams("parallel", "parallel"),
    )(*args)
    return outs if nxt else outs[0]


FFN_ROW_CHUNK = 16
NORM_ROWS = 256
HIST = SUBLANES
DOWN_PANEL = 512
FFN_STAGES = 4


def _ffn_kernel(*refs, seg, tiles_per_seq, nxt):
    (x_ref, gpre_ref, wg_ref, wu_ref, cwg_ref, cwu_ref, cbg_ref, cbu_ref, wd_ref,
     bufg_ref, bufu_ref, gpost_ref) = refs[:12]
    rest = list(refs[12:])
    gnext_ref = rest.pop(0) if nxt else None
    y_ref, sg_ref, su_ref = rest[:3]
    rest = rest[3:]
    h_ref = rest.pop(0) if nxt else None
    xn_ref, u_ref, hm_ref, carry_ref = rest
    i, j = pl.program_id(0), pl.program_id(1)
    nf = pl.num_programs(1)
    tm = x_ref.shape[0]
    tf = wg_ref.shape[1]
    R = FFN_ROW_CHUNK

    norm_rows = [slice(r0, min(r0 + NORM_ROWS, tm)) for r0 in range(0, tm, NORM_ROWS)]

    @pl.when(j == 0)
    def _():
        for rows in norm_rows:
            xn_ref[rows, :] = _rms(x_ref[rows, :], gpre_ref[...]).astype(BF16)
            y_ref[rows, :] = jnp.zeros_like(y_ref[rows, :])

    halves = ((wg_ref, cwg_ref, cbg_ref, bufg_ref, sg_ref), (wu_ref, cwu_ref, cbu_ref, bufu_ref, su_ref))
    rps = tm // FFN_STAGES

    if seg is None:
        for hidx, (_, _, _, buf_ref, _) in enumerate(halves):
            @pl.when((i % tiles_per_seq) == 0)
            def _(buf_ref=buf_ref, hidx=hidx):
                u_ref[hidx, 0:HIST, :] = buf_ref[0]

            @pl.when((i % tiles_per_seq) != 0)
            def _(hidx=hidx):
                u_ref[hidx, 0:HIST, :] = carry_ref[j, hidx]

    def up(a):
        for hidx, (w_ref, _, _, _, _) in enumerate(halves):
            u_ref[hidx, HIST + a:HIST + a + rps, :] = _dot(xn_ref[a:a + rps, :], w_ref[...])

    def gate(a):
        for r0 in range(a, a + rps, R):
            conv = []
            for hidx, (_, cw_ref, cb_ref, buf_ref, _) in enumerate(halves):
                if seg is None:
                    ext = u_ref[hidx, r0:r0 + R + HIST, :]
                    cur = ext[HIST:]
                    s1 = pltpu.roll(ext, 1, axis=0)[HIST:]
                    s2 = pltpu.roll(ext, 2, axis=0)[HIST:]
                else:
                    nsq = R // seg
                    cur = u_ref[hidx, r0 + HIST:r0 + HIST + R, :]
                    b = buf_ref[r0 // seg:r0 // seg + nsq]
                    b0 = jnp.broadcast_to(b[:, 0:1, :], (nsq, seg, tf)).reshape(R, tf)
                    b1 = jnp.broadcast_to(b[:, 1:2, :], (nsq, seg, tf)).reshape(R, tf)
                    t = lax.broadcasted_iota(jnp.int32, (R, 1), 0) & (seg - 1)
                    s1 = jnp.where(t == 0, b1, pltpu.roll(cur, 1, axis=0))
                    s2 = jnp.where(t == 0, b0, jnp.where(t == 1, b1, pltpu.roll(cur, 2, axis=0)))
                conv.append(cw_ref[0:1] * s2 + cw_ref[1:2] * s1 + cw_ref[2:3] * cur + cb_ref[...])
            gate_c, up_c = conv
            hm_ref[r0:r0 + R, :] = (gate_c * jax.nn.sigmoid(gate_c) * up_c).astype(BF16)

    def down(a):
        for n0 in range(0, y_ref.shape[1], DOWN_PANEL):
            n1 = min(n0 + DOWN_PANEL, y_ref.shape[1])
            y_ref[a:a + rps, n0:n1] += _dot(hm_ref[a:a + rps, :], wd_ref[:, n0:n1])

    for st in range(FFN_STAGES):
        up(st * rps)
    for hidx, (_, _, _, _, s_ref) in enumerate(halves):
        if seg is None:
            carry_ref[j, hidx] = u_ref[hidx, tm:tm + HIST, :]
            s_ref[0] = u_ref[hidx, tm:tm + HIST, :]
        else:
            s_ref[...] = u_ref[hidx, HIST:HIST + tm, :].reshape(tm // seg, seg, tf)[:, seg - 2:seg, :]
    for st in range(FFN_STAGES):
        gate(st * rps)
        down(st * rps)

    @pl.when(j == nf - 1)
    def _():
        for rows in norm_rows:
            out = x_ref[rows, :] + _rms(y_ref[rows, :], gpost_ref[...])
            y_ref[rows, :] = out
            if nxt:
                h_ref[rows, :] = _rms(out, gnext_ref[...]).astype(BF16)


def conv_ffn(x, g_pre, g_post, w_up, conv_w, conv_b, w_down, buf, *, layer, seq_len, buf_block0=0, g_next=None,
             tm=1024, tf=512):
    M, D = x.shape
    F = w_down.shape[1]
    nseq = M // seq_len
    tf = _tile(F, tf, LANES)
    nf = F // tf
    nxt = g_next is not None
    if seq_len >= LANES:
        tm = _tile(seq_len, tm, BF16_ROWS)
        seg, tiles_per_seq = None, seq_len // tm
        buf = jnp.pad(buf, ((0, 0), (HIST - 2, 0), (0, 0)))
        buf_spec = lambda half: pl.BlockSpec((1, HIST, tf), lambda i, j: (buf_block0 * nseq + i // tiles_per_seq, 0, half * nf + j))
        st_spec = pl.BlockSpec((1, HIST, tf), lambda i, j: (i, 0, j))
        st_shape = jax.ShapeDtypeStruct((M // tm, HIST, F), F32)
    else:
        tm = _tile(M, tm, max(seq_len, FFN_ROW_CHUNK))
        seg, tiles_per_seq = seq_len, 1
        spt = tm // seq_len
        assert seq_len & (seq_len - 1) == 0 and FFN_ROW_CHUNK % seq_len == 0
        buf_spec = lambda half: pl.BlockSpec((spt, 2, tf), lambda i, j: (buf_block0 * (nseq // spt) + i, 0, half * nf + j))
        st_spec = pl.BlockSpec((spt, 2, tf), lambda i, j: (i, 0, j))
        st_shape = jax.ShapeDtypeStruct((nseq, 2, F), F32)
    assert tm % (FFN_STAGES * FFN_ROW_CHUNK) == 0
    row = lambda a: a.reshape(1, -1).astype(F32)
    vec = pl.BlockSpec((1, D), lambda i, j: (0, 0))
    once = dict(pipeline_mode=pl.Buffered(1))
    in_specs = [
        pl.BlockSpec((tm, D), lambda i, j: (i, 0), **once),
        vec,
        pl.BlockSpec((None, D, tf), lambda i, j: (layer, 0, j)),
        pl.BlockSpec((None, D, tf), lambda i, j: (layer, 0, nf + j)),
        pl.BlockSpec((3, tf), lambda i, j: (0, j)),
        pl.BlockSpec((3, tf), lambda i, j: (0, nf + j)),
        pl.BlockSpec((1, tf), lambda i, j: (0, j)),
        pl.BlockSpec((1, tf), lambda i, j: (0, nf + j)),
        pl.BlockSpec((None, tf, D), lambda i, j: (layer, j, 0)),
        buf_spec(0), buf_spec(1), vec,
    ]
    args = [x, row(g_pre), w_up, w_up, conv_w, conv_w, row(conv_b), row(conv_b), w_down, buf, buf, row(g_post)]
    out_specs = [pl.BlockSpec((tm, D), lambda i, j: (i, 0), **once), st_spec, st_spec]
    out_shape = [jax.ShapeDtypeStruct((M, D), F32), st_shape, st_shape]
    if nxt:
        in_specs.append(vec)
        args.append(row(g_next))
        out_specs.append(pl.BlockSpec((tm, D), lambda i, j: (i, 0), **once))
        out_shape.append(jax.ShapeDtypeStruct((M, D), BF16))
    outs = pl.pallas_call(
        functools.partial(_ffn_kernel, seg=seg, tiles_per_seq=tiles_per_seq, nxt=nxt),
        grid=(M // tm, nf), in_specs=in_specs, out_specs=out_specs, out_shape=out_shape,
        scratch_shapes=[pltpu.VMEM((tm, D), BF16), pltpu.VMEM((2, HIST + tm, tf), F32),
                        pltpu.VMEM((tm, tf), BF16), pltpu.VMEM((nf, 2, HIST, tf), F32)],
        compiler_params=_cparams("arbitrary", "arbitrary"),
    )(*args)
    y, sg, su = outs[:3]
    state = jnp.concatenate([sg, su], axis=-1)
    if seg is None:
        state = state[tiles_per_seq - 1::tiles_per_seq, HIST - 2:]
    return y, state, (outs[3] if nxt else None)


def _rope(x, cos, sin_signed):
    W = x.shape[-1]
    lane = lax.broadcasted_iota(jnp.int32, (1, W), 1)
    swapped = jnp.where((lane & 63) < 32, pltpu.roll(x, W - 32, axis=1), pltpu.roll(x, 32, axis=1))
    return x * cos + swapped * sin_signed


def rope_tables(pos, width, rows):
    inv = ROPE_THETA ** (-jnp.arange(0, 64, 2, dtype=F32) / 64)
    ang = pos.astype(F32)[:, None] * inv[None, :]
    c, s = jnp.cos(ang), jnp.sin(ang)
    cos = jnp.tile(jnp.concatenate([c, c], axis=-1), (rows // pos.shape[0], width // 64))
    sin = jnp.tile(jnp.concatenate([-s, s], axis=-1), (rows // pos.shape[0], width // 64))
    return cos, sin


def _mla_pre_kernel(q_ref, ckv_ref, kpe_ref, gkv_ref, wuk_ref, cos_ref, sin_ref,
                    qcat_ref, kcat_ref, ckvn_ref, kper_ref, *, heads, nope, lora):
    cos, sin = cos_ref[...], sin_ref[...]
    q = q_ref[...]
    qpe = _rope(q[:, heads * nope:], cos, sin)
    for h in range(heads):
        qn = q[:, h * nope:(h + 1) * nope].astype(BF16)
        qcat_ref[h, :, 0:lora] = _dot(qn, wuk_ref[h]).astype(BF16)
        qcat_ref[h, :, lora:lora + LANES] = qpe[:, h * LANES:(h + 1) * LANES].astype(BF16)
    ckvn = _rms(ckv_ref[...], gkv_ref[...])
    kper = _rope(kpe_ref[...], cos[:, :LANES], sin[:, :LANES])
    ckvn_ref[...] = ckvn
    kper_ref[...] = kper
    kcat_ref[:, 0:lora] = ckvn.astype(BF16)
    kcat_ref[:, lora:lora + LANES] = kper.astype(BF16)


def mla_pre(q, z, g_kv, w_ukT, cos, sin, *, ckv_col, kpe_col, tm=256):
    M = q.shape[0]
    heads, nope, lora = w_ukT.shape
    tm = _tile(M, min(tm, cos.shape[0]), BF16_ROWS)
    ntab = cos.shape[0] // tm
    wq = q.shape[1]
    kw = lora + LANES
    return pl.pallas_call(
        functools.partial(_mla_pre_kernel, heads=heads, nope=nope, lora=lora),
        grid=(M // tm,),
        in_specs=[
            pl.BlockSpec((tm, wq), lambda i: (i, 0)),
            pl.BlockSpec((tm, lora), lambda i: (i, ckv_col)),
            pl.BlockSpec((tm, LANES), lambda i: (i, kpe_col)),
            pl.BlockSpec((1, lora), lambda i: (0, 0)),
            pl.BlockSpec((heads, nope, lora), lambda i: (0, 0, 0)),
            pl.BlockSpec((tm, heads * LANES), lambda i: (i % ntab, 0)),
            pl.BlockSpec((tm, heads * LANES), lambda i: (i % ntab, 0)),
        ],
        out_specs=[
            pl.BlockSpec((heads, tm, kw), lambda i: (0, i, 0)),
            pl.BlockSpec((tm, kw), lambda i: (i, 0)),
            pl.BlockSpec((tm, lora), lambda i: (i, 0)),
            pl.BlockSpec((tm, LANES), lambda i: (i, 0)),
        ],
        out_shape=[jax.ShapeDtypeStruct((heads, M, kw), BF16),
                   jax.ShapeDtypeStruct((M, kw), BF16),
                   jax.ShapeDtypeStruct((M, lora), F32),
                   jax.ShapeDtypeStruct((M, LANES), F32)],
        compiler_params=_cparams("parallel"),
    )(q, z, z, g_kv.reshape(1, lora).astype(F32), w_ukT, cos, sin)


def _mla_prompt_kernel(q_ref, k_ref, wuv_ref, y_ref, m_ref, l_ref, acc_ref, *, scale, lora, vdim):
    qi, ki = pl.program_id(1), pl.program_id(2)
    heads, tq, kw = q_ref.shape
    tk = k_ref.shape[0]

    @pl.when(ki == 0)
    def _():
        m_ref[...] = jnp.full_like(m_ref, NEG_INF)
        l_ref[...] = jnp.zeros_like(l_ref)
        acc_ref[...] = jnp.zeros_like(acc_ref)

    def step(masked):
        q = q_ref[...].reshape(heads * tq, kw)
        k = k_ref[...]
        s = _dot_nt(q, k) * scale
        if masked:
            qpos = qi * tq + (lax.broadcasted_iota(jnp.int32, (heads * tq, 1), 0) & (tq - 1))
            kpos = ki * tk + lax.broadcasted_iota(jnp.int32, (1, tk), 1)
            s = jnp.where(kpos <= qpos, s, NEG_INF)
        m_old = m_ref[...]
        m_new = jnp.maximum(m_old, jnp.max(s, axis=-1, keepdims=True))
        alpha = jnp.exp(m_old - m_new)
        p = jnp.exp(s - _lane_tile(m_new, tk // LANES))
        l_ref[...] = alpha * l_ref[...] + jnp.sum(p, axis=-1, keepdims=True)
        acc_ref[...] = _lane_tile(alpha, lora // LANES) * acc_ref[...] + _dot(p.astype(BF16), k[:, :lora])
        m_ref[...] = m_new

    needed = ki * tk <= qi * tq + tq - 1
    unmasked = ki * tk + tk - 1 <= qi * tq

    @pl.when(needed & unmasked)
    def _():
        step(False)

    @pl.when(needed & jnp.logical_not(unmasked))
    def _():
        step(True)

    @pl.when(ki == pl.num_programs(2) - 1)
    def _():
        o = (acc_ref[...] / _lane_tile(l_ref[...], lora // LANES)).astype(BF16)
        for h in range(heads):
            y_ref[:, h * vdim:(h + 1) * vdim] = _dot(o[h * tq:(h + 1) * tq], wuv_ref[:, h * vdim:(h + 1) * vdim])


def mla_attn_prompt(qcat, kcat, w_uv, *, batch, scale, tq=128, tk=512):
    heads, M, kw = qcat.shape
    S = M // batch
    lora, hv = w_uv.shape
    vdim = hv // heads
    tq = _tile(S, tq, BF16_ROWS)
    tk = _tile(S, tk, LANES)
    assert tq & (tq - 1) == 0 and tk % LANES == 0 and lora % LANES == 0
    nq, nk = S // tq, S // tk

    def kmap(b, qi, ki):
        return (b * nk + jnp.minimum(ki, (qi * tq + tq - 1) // tk), 0)

    return pl.pallas_call(
        functools.partial(_mla_prompt_kernel, scale=scale, lora=lora, vdim=vdim),
        grid=(batch, nq, nk),
        in_specs=[
            pl.BlockSpec((heads, tq, kw), lambda b, qi, ki: (0, b * nq + qi, 0)),
            pl.BlockSpec((tk, kw), kmap),
            pl.BlockSpec((lora, hv), lambda b, qi, ki: (0, 0)),
        ],
        out_specs=pl.BlockSpec((tq, hv), lambda b, qi, ki: (b * nq + qi, 0)),
        out_shape=jax.ShapeDtypeStruct((M, hv), F32),
        scratch_shapes=[pltpu.VMEM((heads * tq, LANES), F32), pltpu.VMEM((heads * tq, LANES), F32),
                        pltpu.VMEM((heads * tq, lora), F32)],
        compiler_params=_cparams("parallel", "parallel", "arbitrary"),
    )(qcat, kcat, w_uv)


def _mla_sample_kernel(pt_ref, q_ref, ckvn_ref, kper_ref, wuv_ref, poolc_ref, poolp_ref, y_ref,
                       kbuf_ref, pbuf_ref, knew_ref, sem_ref,
                       *, layer, scale, lora, rope, vdim, seq, n_pages, page, chunk):
    b = pl.program_id(0)
    slot = b % 2
    rows = q_ref.shape[1]

    def page_copies(seq_idx, sl, p):
        src_page = pt_ref[seq_idx, p]
        dst = pl.ds(pl.multiple_of(p * page, page), page)
        return (pltpu.make_async_copy(poolc_ref.at[layer, src_page], kbuf_ref.at[sl, dst], sem_ref.at[0, sl]),
                pltpu.make_async_copy(poolp_ref.at[layer, src_page], pbuf_ref.at[sl, p], sem_ref.at[1, sl]))

    def for_pages(seq_idx, sl, start):
        def body(i, carry):
            for k in range(DMA_QUEUES):
                for cp in page_copies(seq_idx, sl, i * DMA_QUEUES + k):
                    if start:
                        cp.start(priority=k)
                    else:
                        cp.wait()
            return carry
        lax.fori_loop(0, n_pages // DMA_QUEUES, body, 0)

    @pl.when(b == 0)
    def _():
        for_pages(0, 0, True)

    @pl.when(b + 1 < pl.num_programs(0))
    def _():
        for_pages(b + 1, 1 - slot, True)

    for_pages(b, slot, False)

    q = q_ref[0]
    q_lat, q_pe = q[:, :lora], q[:, lora:lora + rope]
    scores, vals = [], []
    for c0 in range(0, n_pages * page, chunk):
        kc = kbuf_ref[slot, c0:c0 + chunk, :].astype(BF16)
        kp_t = jnp.concatenate([pbuf_ref[slot, p] for p in range(c0 // page, (c0 + chunk) // page)], axis=1)
        scores.append((_dot_nt(q_lat, kc) + _dot(q_pe, kp_t.astype(BF16))) * scale)
        vals.append(kc)
    knew_ref[...] = jnp.zeros_like(knew_ref)
    knew_ref[0:seq, 0:lora] = ckvn_ref[...]
    knew_ref[0:seq, lora:lora + LANES] = kper_ref[...]
    kn = knew_ref[...].astype(BF16)
    qidx = lax.broadcasted_iota(jnp.int32, (rows, 1), 0) & (seq - 1)
    kidx = lax.broadcasted_iota(jnp.int32, (1, page), 1)
    scores.append(jnp.where(kidx <= qidx, _dot_nt(q, kn) * scale, NEG_INF))
    vals.append(kn[:, :lora])
    m = functools.reduce(jnp.maximum, [jnp.max(s, axis=-1, keepdims=True) for s in scores])
    probs = [jnp.exp(s - m) for s in scores]
    l = functools.reduce(jnp.add, [jnp.sum(p, axis=-1, keepdims=True) for p in probs])
    acc = functools.reduce(jnp.add, [_dot(p.astype(BF16), v) for p, v in zip(probs, vals)])
    o = (acc / l).astype(BF16)
    full = _dot(o, wuv_ref[...])
    for h in range(rows // seq):
        y_ref[:, h * vdim:(h + 1) * vdim] = full[h * seq:(h + 1) * seq, h * vdim:(h + 1) * vdim]


def mla_attn_sample(qs, pool_ckv, pool_kpe_t, layer, page_table, ckv_n, kpe_r, w_uv, *, scale, chunk_pages=16):
    B, rows, kw = qs.shape
    _, _, page, lora = pool_ckv.shape
    rope = pool_kpe_t.shape[2]
    n_pages = page_table.shape[1]
    chunk = _tile(n_pages, chunk_pages, 1) * page
    hv = w_uv.shape[1]
    seq = ckv_n.shape[0] // B
    vdim = hv // (rows // seq)
    assert seq & (seq - 1) == 0 and seq <= page and n_pages % DMA_QUEUES == 0
    return pl.pallas_call(
        functools.partial(_mla_sample_kernel, layer=layer, scale=scale, lora=lora, rope=rope, vdim=vdim, seq=seq,
                          n_pages=n_pages, page=page, chunk=chunk),
        grid_spec=pltpu.PrefetchScalarGridSpec(
            num_scalar_prefetch=1, grid=(B,),
            in_specs=[pl.BlockSpec((1, rows, kw), lambda b, pt: (b, 0, 0)),
                      pl.BlockSpec((seq, lora), lambda b, pt: (b, 0)),
                      pl.BlockSpec((seq, LANES), lambda b, pt: (b, 0)),
                      pl.BlockSpec((lora, hv), lambda b, pt: (0, 0)),
                      pl.BlockSpec(memory_space=pl.ANY),
                      pl.BlockSpec(memory_space=pl.ANY)],
            out_specs=pl.BlockSpec((seq, hv), lambda b, pt: (b, 0)),
            scratch_shapes=[pltpu.VMEM((2, n_pages * page, lora), F32),
                            pltpu.VMEM((2, n_pages, rope, page), F32),
                            pltpu.VMEM((page, kw), F32),
                            pltpu.SemaphoreType.DMA((2, 2))]),
        out_shape=jax.ShapeDtypeStruct((B * seq, hv), F32),
        compiler_params=_cparams("arbitrary"),
    )(page_table, qs, ckv_n, kpe_r, w_uv, pool_ckv, pool_kpe_t)


def _dwconv_kernel(*refs, mode, taps, hist):
    if mode == "conf":
        a_ref, b_ref, st_ref, w_ref, cb_ref, lg_ref, lb_ref, y_ref, so_ref, buf_ref, c_ref = refs
    else:
        gate_ref, a_ref, b_ref, st_ref, w_ref, y_ref, so_ref, buf_ref, c_ref = refs
    t = pl.program_id(1)
    nseq = st_ref.shape[0]
    tt, C = a_ref.shape[0] // nseq, a_ref.shape[1]
    rc = min(tt, 64)
    for n in range(nseq):
        rows = slice(n * tt, (n + 1) * tt)

        @pl.when(t == 0)
        def _(n=n):
            buf_ref[n, 0:hist] = st_ref[n]

        if mode == "conf":
            u = a_ref[rows, :] * jax.nn.sigmoid(b_ref[rows, :])
        else:
            u = a_ref[rows, :] * b_ref[rows, :]
        buf_ref[n, hist:hist + tt] = u
        span = rc + hist
        for c0 in range(0, C, LANES):
            for r0 in range(0, tt, rc):
                acc = None
                if taps > SUBLANES:
                    ext = buf_ref[n, r0:r0 + span, c0:c0 + LANES]
                    rot = [ext] + [pltpu.roll(ext, span - s, axis=0) for s in range(1, SUBLANES)]
                    for k in range(taps):
                        first = hist - (taps - 1) + k
                        a, s = divmod(first, SUBLANES)
                        term = w_ref[k:k + 1, c0:c0 + LANES] * rot[s][a * SUBLANES:a * SUBLANES + rc]
                        acc = term if acc is None else acc + term
                else:
                    for k in range(taps):
                        first = r0 + hist - (taps - 1) + k
                        term = w_ref[k:k + 1, c0:c0 + LANES] * buf_ref[n, first:first + rc, c0:c0 + LANES]
                        acc = term if acc is None else acc + term
                c_ref[n, r0:r0 + rc, c0:c0 + LANES] = acc
        c = c_ref[n]
        if mode == "conf":
            c = c + cb_ref[...]
            mu = jnp.mean(c, axis=-1, keepdims=True)
            d = c - mu
            var = jnp.mean(d * d, axis=-1, keepdims=True)
            yn = d * lax.rsqrt(var + EPS) * lg_ref[...] + lb_ref[...]
            y_ref[rows, :] = yn * jax.nn.sigmoid(yn)
        else:
            y_ref[rows, :] = gate_ref[rows, :] * c
        so_ref[n] = buf_ref[n, tt:tt + hist]
        if tt >= hist:
            buf_ref[n, 0:hist] = buf_ref[n, tt:tt + hist]


def dwconv(z, state, w, *, mode, cols, batch, extra=(), tt=128, nseq=8):
    M = z.shape[0]
    taps, C = w.shape
    S = M // batch
    hist = -(-(taps - 1) // SUBLANES) * SUBLANES
    tt = _tile(S, tt)
    nt = S // tt
    assert nt == 1 or tt >= hist
    nseq = _tile(batch, nseq, 1) if nt == 1 and tt < hist + SUBLANES else 1
    st = jnp.pad(state.astype(F32), ((0, 0), (hist - (taps - 1), 0), (0, 0)))
    colspec = lambda c: pl.BlockSpec((nseq * tt, C), lambda b, t: (b * nt + t, c))
    vec = pl.BlockSpec((1, C), lambda b, t: (0, 0))
    in_specs = [colspec(c) for c in cols]
    in_specs += [pl.BlockSpec((nseq, hist, C), lambda b, t: (b, 0, 0)), pl.BlockSpec((taps, C), lambda b, t: (0, 0))]
    in_specs += [vec] * len(extra)
    y, so = pl.pallas_call(
        functools.partial(_dwconv_kernel, mode=mode, taps=taps, hist=hist),
        grid=(batch // nseq, nt),
        in_specs=in_specs,
        out_specs=[pl.BlockSpec((nseq * tt, C), lambda b, t: (b * nt + t, 0)),
                   pl.BlockSpec((nseq, hist, C), lambda b, t: (b, 0, 0))],
        out_shape=[jax.ShapeDtypeStruct((M, C), F32), jax.ShapeDtypeStruct((batch, hist, C), F32)],
        scratch_shapes=[pltpu.VMEM((nseq, hist + tt, C), F32), pltpu.VMEM((nseq, tt, C), F32)],
        compiler_params=_cparams("parallel", "arbitrary"),
    )(*([z] * len(cols)), st, w.astype(F32), *[e.reshape(1, C).astype(F32) for e in extra])
    return y, so[:, hist - (taps - 1):]


def _swa_rope_kernel(q_ref, k_ref, cos_ref, sin_ref, qo_ref, ko_ref):
    cos, sin = cos_ref[...], sin_ref[...]
    kw = k_ref.shape[1]
    qo_ref[...] = _rope(q_ref[...], cos, sin)
    ko_ref[...] = _rope(k_ref[...], cos[:, :kw], sin[:, :kw])


def swa_rope(z, cos, sin, *, q_col, k_col, qw, kw, tm=256):
    M = z.shape[0]
    tm = _tile(M, min(tm, cos.shape[0]))
    ntab = cos.shape[0] // tm
    return pl.pallas_call(
        _swa_rope_kernel,
        grid=(M // tm,),
        in_specs=[pl.BlockSpec((tm, qw), lambda i: (i, q_col)),
                  pl.BlockSpec((tm, kw), lambda i: (i, k_col)),
                  pl.BlockSpec((tm, qw), lambda i: (i % ntab, 0)),
                  pl.BlockSpec((tm, qw), lambda i: (i % ntab, 0))],
        out_specs=[pl.BlockSpec((tm, qw), lambda i: (i, 0)), pl.BlockSpec((tm, kw), lambda i: (i, 0))],
        out_shape=[jax.ShapeDtypeStruct((M, qw), F32), jax.ShapeDtypeStruct((M, kw), F32)],
        compiler_params=_cparams("parallel"),
    )(z, z, cos, sin)


def _sink_softmax_pv(s, sink, v):
    m = jnp.maximum(jnp.max(s, axis=-1, keepdims=True), sink)
    p = jnp.exp(s - m)
    denom = jnp.sum(p, axis=-1, keepdims=True) + jnp.exp(sink - m)
    return _dot(p.astype(BF16), v) / denom


def _swa_prompt_kernel(sink_ref, q_ref, kc_ref, kp_ref, vc_ref, vp_ref, y_ref, *, scale, hd, group):
    n = pl.program_id(1)
    W = q_ref.shape[0]
    kvh = kc_ref.shape[1] // hd
    qi = lax.broadcasted_iota(jnp.int32, (W, 1), 0)
    si = lax.broadcasted_iota(jnp.int32, (1, 2 * W), 1)
    rel = qi + W - si
    valid = (rel >= 0) & (rel < W) & (n * W + si - W >= 0)
    q = q_ref[...].astype(BF16)
    kk = jnp.concatenate([kp_ref[...], kc_ref[...]], axis=0).astype(BF16)
    vv = jnp.concatenate([vp_ref[...], vc_ref[...]], axis=0).astype(BF16)
    outs = []
    for kv in range(kvh):
        k_h = kk[:, kv * hd:(kv + 1) * hd]
        v_h = vv[:, kv * hd:(kv + 1) * hd]
        for g in range(group):
            h = kv * group + g
            s = _dot_nt(q[:, h * hd:(h + 1) * hd], k_h) * scale
            s = jnp.where(valid, s, NEG_INF)
            outs.append(_sink_softmax_pv(s, sink_ref[h], v_h))
    y_ref[...] = jnp.concatenate(outs, axis=-1)


def swa_prompt(q, k, z, sinks, *, v_col, batch, scale, hd):
    M, qw = q.shape
    kw = k.shape[1]
    S = M // batch
    W = WINDOW
    nb = S // W
    group = qw // kw
    cur = lambda b, n: b * nb + n
    prev = lambda b, n: b * nb + jnp.maximum(n - 1, 0)
    return pl.pallas_call(
        functools.partial(_swa_prompt_kernel, scale=scale, hd=hd, group=group),
        grid=(batch, nb),
        in_specs=[pl.BlockSpec(memory_space=pltpu.SMEM),
                  pl.BlockSpec((W, qw), lambda b, n: (cur(b, n), 0)),
                  pl.BlockSpec((W, kw), lambda b, n: (cur(b, n), 0)),
                  pl.BlockSpec((W, kw), lambda b, n: (prev(b, n), 0)),
                  pl.BlockSpec((W, kw), lambda b, n: (cur(b, n), v_col)),
                  pl.BlockSpec((W, kw), lambda b, n: (prev(b, n), v_col))],
        out_specs=pl.BlockSpec((W, qw), lambda b, n: (cur(b, n), 0)),
        out_shape=jax.ShapeDtypeStruct((M, qw), F32),
        compiler_params=_cparams("parallel", "parallel"),
    )(sinks.astype(F32), q, k, k, z, z)


def _swa_sample_kernel(sink_ref, q_ref, kb_ref, vb_ref, kn_ref, vn_ref, y_ref, ko_ref, vo_ref,
                       ks_ref, vs_ref, *, scale, hd, group, first_pos, seq):
    nseq, nbuf = kb_ref.shape[0], kb_ref.shape[1]
    npad = ks_ref.shape[0] // nseq
    kvh = kn_ref.shape[1] // hd
    for n in range(nseq):
        rows = slice(n * seq, (n + 1) * seq)
        base = n * npad
        for s_ref, b_ref, n_ref, o_ref in ((ks_ref, kb_ref, kn_ref, ko_ref), (vs_ref, vb_ref, vn_ref, vo_ref)):
            s_ref[base:base + nbuf] = b_ref[n]
            s_ref[base + nbuf:base + nbuf + seq] = n_ref[rows, :]
            s_ref[base + nbuf + seq:base + npad] = jnp.zeros((npad - nbuf - seq, s_ref.shape[1]), F32)
            o_ref[n] = s_ref[base + seq:base + seq + nbuf]
    kk = ks_ref[...].astype(BF16)
    vv = vs_ref[...].astype(BF16)
    r = lax.broadcasted_iota(jnp.int32, (group * nseq * seq, 1), 0)
    c = lax.broadcasted_iota(jnp.int32, (1, nseq * npad), 1)
    q_seq, q_i = _div_pow2(r, seq) & (nseq - 1), r & (seq - 1)
    k_seq, k_j = _div_pow2(c, npad), c & (npad - 1)
    rel = q_i + nbuf - k_j
    valid = (q_seq == k_seq) & (rel >= 0) & (rel < WINDOW) & (first_pos + k_j >= 0)
    q = q_ref[...]
    outs = []
    for kv in range(kvh):
        heads = range(kv * group, (kv + 1) * group)
        q4 = jnp.concatenate([q[:, h * hd:(h + 1) * hd] for h in heads], axis=0)
        sink = jnp.concatenate([jnp.full((nseq * seq, 1), sink_ref[h], F32) for h in heads], axis=0)
        s = _dot_nt(q4.astype(BF16), kk[:, kv * hd:(kv + 1) * hd]) * scale
        o = _sink_softmax_pv(jnp.where(valid, s, NEG_INF), sink, vv[:, kv * hd:(kv + 1) * hd])
        outs += [o[g * nseq * seq:(g + 1) * nseq * seq] for g in range(group)]
    y_ref[...] = jnp.concatenate(outs, axis=-1)


def swa_sample(q, k_new, z, buf_k, buf_v, buf_block0, sinks, *, v_col, batch, scale, hd, past_len, nseq=8):
    M, qw = q.shape
    _, nbuf, kw = buf_k.shape
    seq = M // batch
    group = qw // kw
    npad = 2 * nbuf
    nseq = _tile(batch, nseq, 1)
    steps = batch // nseq
    assert seq & (seq - 1) == 0 and nseq & (nseq - 1) == 0 and nbuf + seq <= npad and nbuf == WINDOW
    row = lambda b: (b, 0)
    bufspec = pl.BlockSpec((nseq, nbuf, kw), lambda b: (buf_block0 * steps + b, 0, 0))
    outbuf = pl.BlockSpec((nseq, nbuf, kw), lambda b: (b, 0, 0))
    return pl.pallas_call(
        functools.partial(_swa_sample_kernel, scale=scale, hd=hd, group=group, first_pos=past_len - nbuf, seq=seq),
        grid=(steps,),
        in_specs=[pl.BlockSpec(memory_space=pltpu.SMEM),
                  pl.BlockSpec((nseq * seq, qw), row),
                  bufspec, bufspec,
                  pl.BlockSpec((nseq * seq, kw), row),
                  pl.BlockSpec((nseq * seq, kw), lambda b: (b, v_col))],
        out_specs=[pl.BlockSpec((nseq * seq, qw), row), outbuf, outbuf],
        out_shape=[jax.ShapeDtypeStruct((M, qw), F32),
                   jax.ShapeDtypeStruct((batch, nbuf, kw), F32),
                   jax.ShapeDtypeStruct((batch, nbuf, kw), F32)],
        scratch_shapes=[pltpu.VMEM((nseq * npad, kw), F32), pltpu.VMEM((nseq * npad, kw), F32)],
        compiler_params=_cparams("parallel"),
    )(sinks.astype(F32), q, buf_k, buf_v, k_new, z)


def _mem_attn_kernel(q_ref, k_ref, v_ref, o_ref, *, scale, hd, nseq, head_rows):
    heads = q_ref.shape[1] // hd
    rows = q_ref.shape[0]
    nk = k_ref.shape[0] // heads if head_rows else k_ref.shape[0]
    q = q_ref[...].astype(BF16)
    if nseq > 1:
        q_seq = _div_pow2(lax.broadcasted_iota(jnp.int32, (rows, 1), 0), rows // nseq)
        k_seq = _div_pow2(lax.broadcasted_iota(jnp.int32, (1, nk), 1), nk // nseq)
        same = q_seq == k_seq
    outs = []
    for h in range(heads):
        sl = slice(h * hd, (h + 1) * hd)
        if head_rows:
            k_h = k_ref[pl.ds(h, nk, stride=heads), :].astype(BF16)
            v_h = v_ref[pl.ds(h, nk, stride=heads), :].astype(BF16)
        else:
            k_h = k_ref[:, sl].astype(BF16)
            v_h = v_ref[:, sl].astype(BF16)
        s = _dot_nt(q[:, sl], k_h) * scale
        if nseq > 1:
            s = jnp.where(same, s, NEG_INF)
        m = jnp.max(s, axis=-1, keepdims=True)
        p = jnp.exp(s - m)
        p = p / jnp.sum(p, axis=-1, keepdims=True)
        outs.append(_dot(p.astype(BF16), v_h))
    o_ref[...] = jnp.concatenate(outs, axis=-1)


def mem_attn(q, k_arr, v_arr, *, k_col=0, v_col=0, kv_block0=0, head_rows=False, batch, mem_tokens, scale, hd,
             tq=256, nseq=8):
    M, w = q.shape
    heads = w // hd
    S = M // batch
    tq = _tile(S, tq)
    nq = S // tq
    nseq = _tile(batch, nseq, 1) if nq == 1 else 1
    nb = batch // nseq
    if head_rows:
        kv_spec = lambda col: pl.BlockSpec((nseq * mem_tokens * heads, hd), lambda b, i: (kv_block0 * nb + b, 0))
    else:
        kv_spec = lambda col: pl.BlockSpec((nseq * mem_tokens, w), lambda b, i: (kv_block0 * nb + b, col))
    return pl.pallas_call(
        functools.partial(_mem_attn_kernel, scale=scale, hd=hd, nseq=nseq, head_rows=head_rows),
        grid=(nb, nq),
        in_specs=[pl.BlockSpec((nseq * tq, w), lambda b, i: (b * nq + i, 0)), kv_spec(k_col), kv_spec(v_col)],
        out_specs=pl.BlockSpec((nseq * tq, w), lambda b, i: (b * nq + i, 0)),
        out_shape=jax.ShapeDtypeStruct((M, w), F32),
        compiler_params=_cparams("parallel", "parallel"),
    )(q, k_arr, v_arr)


def _prep_weights(p):
    dims = p["dims"]
    H, nope, rope, lora, qlora = dims["mla_heads"], dims["nope"], dims["rope"], dims["lora"], dims["qlora"]
    w = {}
    wi = p["w_even_in"]
    n_in = wi.shape[-1]
    pad = jnp.zeros(wi.shape[:2] + (-(-(n_in + LANES - rope) // 512) * 512 - n_in,), wi.dtype)
    w["even_in"] = jnp.concatenate(
        [wi[..., qlora + lora + rope:], wi[..., :qlora + lora + rope], pad], axis=-1).astype(BF16)
    wq = p["w_q_b"].reshape(p["w_q_b"].shape[0], qlora, H, nope + rope)
    wq_pe = jnp.pad(wq[..., nope:], ((0, 0), (0, 0), (0, 0), (0, LANES - rope)))
    w["q_b"] = jnp.concatenate([wq[..., :nope].reshape(-1, qlora, H * nope),
                                wq_pe.reshape(-1, qlora, H * LANES)], axis=-1).astype(BF16)
    w["ukT"] = jnp.transpose(p["w_uk"], (0, 2, 3, 1)).astype(BF16)
    w["uv"] = p["w_uv"].reshape(p["w_uv"].shape[0], lora, -1).astype(BF16)
    for name in ("w_even_out", "w_odd_in", "w_odd_out", "w_mem_q", "w_mem_kv", "w_mem_o", "w_ffn_up", "w_ffn_down"):
        w[name] = p[name].astype(BF16)
    return w


def _trunk(x, pos, p, w, mem, past, page_table):
    dims = p["dims"]
    B, S, D = x.shape
    M = B * S
    H, nope, rope, lora, qlora = dims["mla_heads"], dims["nope"], dims["rope"], dims["lora"], dims["qlora"]
    conf_ch, sc_ch = dims["conf_ch"], dims["sc_ch"]
    swa_h, swa_kv, swa_hd = dims["swa_heads"], dims["swa_kv"], dims["swa_hd"]
    mem_h, mem_hd = dims["mem_heads"], dims["mem_hd"]
    mem_w = mem_h * mem_hd
    ff = dims["d_ff"]
    depth = p["norm_gains"].shape[0]
    prompt = past is None
    x = x.reshape(M, D)
    tab_rows = S if prompt else max(S, min(M, 256))
    cos, sin = rope_tables(pos, max(H * LANES, swa_h * swa_hd), tab_rows)
    new = {k: [] for k in ("mla_ckv", "mla_kpe", "swa_k", "swa_v", "mem_k", "mem_v", "conf_conv", "sc_conv", "ffn_conv")}
    if not prompt:
        mem_k_all = past["mem_k"].reshape(-1, mem_hd)
        mem_v_all = past["mem_v"].reshape(-1, mem_hd)
        kpe_pool_t = jnp.swapaxes(past["mla_kpe"], 2, 3)
        swa_k_all = past["swa_k"].reshape((-1,) + past["swa_k"].shape[2:3] + (swa_kv * swa_hd,))
        swa_v_all = past["swa_v"].reshape(swa_k_all.shape)
        ffn_all = past["ffn_conv"].reshape((-1,) + past["ffn_conv"].shape[2:])
    h = None
    for layer in range(depth):
        g = p["norm_gains"][layer]
        w_in = w["even_in"][layer // 2] if layer % 2 == 0 else w["w_odd_in"][layer // 2]
        z = matmul_fused(x, w_in, g_pre=g[0]) if h is None else matmul_fused(h, w_in, tm=512)
        if layer % 2 == 0:
            i = layer // 2
            glu_w = 2 * conf_ch
            q = matmul_fused(z, w["q_b"][i], g_pre=p["g_q_a"][i], xcol=glu_w // qlora)
            qcat, kcat, ckv_n, kpe_r = mla_pre(
                q, z, p["g_kv_a"][i], w["ukT"][i], cos[:, :H * LANES], sin[:, :H * LANES],
                ckv_col=(glu_w + qlora) // lora, kpe_col=(glu_w + qlora + lora) // LANES)
            scale = (nope + rope) ** -0.5
            if prompt:
                y_mla = mla_attn_prompt(qcat, kcat, w["uv"][i], batch=B, scale=scale)
                conf_state = jnp.zeros((B, p["conf_dw_w"].shape[1] - 1, conf_ch), F32)
            else:
                qs = qcat.reshape(H, B, S, lora + LANES).transpose(1, 0, 2, 3).reshape(B, H * S, lora + LANES)
                y_mla = mla_attn_sample(qs, past["mla_ckv"], kpe_pool_t, i, page_table,
                                        ckv_n, kpe_r, w["uv"][i], scale=scale)
                conf_state = past["conf_conv"][i]
            c, conf_new = dwconv(z, conf_state, p["conf_dw_w"][i], mode="conf", cols=(0, 1), batch=B,
                                 extra=(p["conf_dw_b"][i], p["conf_ln_g"][i], p["conf_ln_b"][i]))
            x, hq = matmul_fused(y_mla, w["w_even_out"][i], x2=c, g_post=g[1], res=x, g_next=g[2])
            new["mla_ckv"].append(ckv_n.reshape(B, S, lora))
            new["mla_kpe"].append(kpe_r[:, :rope].reshape(B, S, rope))
            new["conf_conv"].append(conf_new)
        else:
            j = layer // 2
            qw, kw = swa_h * swa_hd, swa_kv * swa_hd
            sc_state = jnp.zeros((B, p["sc_conv_w"].shape[1] - 1, sc_ch), F32) if prompt else past["sc_conv"][j]
            y_sc, sc_new = dwconv(z, sc_state, p["sc_conv_w"][j], mode="sc", cols=(0, 1, 2), batch=B)
            q_r, k_r = swa_rope(z, cos[:, :qw], sin[:, :qw], q_col=3 * sc_ch // qw, k_col=(3 * sc_ch + qw) // kw,
                                qw=qw, kw=kw)
            v_col = (3 * sc_ch + qw + kw) // kw
            scale = swa_hd ** -0.5
            if prompt:
                y_att = swa_prompt(q_r, k_r, z, p["swa_sinks"][j], v_col=v_col, batch=B, scale=scale, hd=swa_hd)
                v = z[:, v_col * kw:(v_col + 1) * kw]
                nbuf = min(WINDOW, S)
                nk = k_r.reshape(B, S, swa_kv, swa_hd)[:, S - nbuf:]
                nv = v.reshape(B, S, swa_kv, swa_hd)[:, S - nbuf:]
            else:
                nbuf = swa_k_all.shape[1]
                y_att, nk, nv = swa_sample(q_r, k_r, z, swa_k_all, swa_v_all, j, p["swa_sinks"][j], v_col=v_col,
                                           batch=B, scale=scale, hd=swa_hd, past_len=past["past_len"])
                nk = nk.reshape(B, nbuf, swa_kv, swa_hd)
                nv = nv.reshape(B, nbuf, swa_kv, swa_hd)
            x, hq = matmul_fused(y_sc, w["w_odd_out"][j], x2=y_att, g_post=g[1], res=x, g_next=g[2])
            new["swa_k"].append(nk)
            new["swa_v"].append(nv)
            new["sc_conv"].append(sc_new)
        qm = matmul_fused(hq, w["w_mem_q"][layer], tm=512)
        if prompt:
            mem_tokens = mem.shape[1]
            kv = matmul_fused(mem.reshape(-1, D), w["w_mem_kv"][layer])
            o = mem_attn(qm, kv, kv, k_col=0, v_col=1, kv_block0=0, batch=B, mem_tokens=mem_tokens,
                         scale=mem_hd ** -0.5, hd=mem_hd)
            kv = kv.reshape(B, mem_tokens, 2, mem_h, mem_hd)
            new["mem_k"].append(kv[:, :, 0])
            new["mem_v"].append(kv[:, :, 1])
        else:
            o = mem_attn(qm, mem_k_all, mem_v_all, head_rows=True, kv_block0=layer, batch=B,
                         mem_tokens=past["mem_k"].shape[2], scale=mem_hd ** -0.5, hd=mem_hd)
        x = matmul_fused(o, w["w_mem_o"][layer], g_post=g[3], res=x)
        g_next = p["norm_gains"][layer + 1][0] if layer + 1 < depth else None
        if prompt:
            ffn_buf, block0 = jnp.zeros((B, 2, 2 * ff), F32), 0
        else:
            ffn_buf, block0 = ffn_all, layer
        x, ffn_new, h = conv_ffn(x, g[4], g[5], w["w_ffn_up"], p["ffn_conv_w"][layer], p["ffn_conv_b"][layer],
                                 w["w_ffn_down"], ffn_buf, layer=layer, seq_len=S, buf_block0=block0, g_next=g_next)
        new["ffn_conv"].append(ffn_new)
    return x.reshape(B, S, D), {k: jnp.stack(v) for k, v in new.items() if v}


def kernel(x_prompt, x_sample, cache_mla_ckv, cache_mla_kpe, cache_swa_k, cache_swa_v, cache_mem_k, cache_mem_v,
           state_conf_conv, state_sc_conv, state_ffn_conv, page_table, mem_prompt, norm_gains, w_even_in, g_q_a,
           w_q_b, g_kv_a, w_uk, w_uv, conf_dw_w, conf_dw_b, conf_ln_g, conf_ln_b, w_even_out, w_odd_in, sc_conv_w,
           swa_sinks, w_odd_out, w_mem_q, w_mem_kv, w_mem_o, w_ffn_up, ffn_conv_w, ffn_conv_b, w_ffn_down):
    lora, heads, nope = w_uk.shape[1], w_uk.shape[2], w_uk.shape[3]
    rope = cache_mla_kpe.shape[-1]
    dims = dict(mla_heads=heads, nope=nope, rope=rope, lora=lora, qlora=g_q_a.shape[1],
                conf_ch=conf_dw_w.shape[2], sc_ch=sc_conv_w.shape[2],
                swa_heads=swa_sinks.shape[1], swa_kv=cache_swa_k.shape[3], swa_hd=cache_swa_k.shape[4],
                mem_heads=cache_mem_k.shape[3], mem_hd=cache_mem_k.shape[4], d_ff=w_ffn_down.shape[1])
    assert rope == 64 and dims["swa_hd"] == 64, "rotary helper assumes 64-wide rotary heads"
    p = dict(dims=dims, norm_gains=norm_gains, w_even_in=w_even_in, g_q_a=g_q_a, w_q_b=w_q_b, g_kv_a=g_kv_a,
             w_uk=w_uk, w_uv=w_uv, conf_dw_w=conf_dw_w, conf_dw_b=conf_dw_b, conf_ln_g=conf_ln_g,
             conf_ln_b=conf_ln_b, w_even_out=w_even_out, w_odd_in=w_odd_in, sc_conv_w=sc_conv_w,
             swa_sinks=swa_sinks, w_odd_out=w_odd_out, w_mem_q=w_mem_q, w_mem_kv=w_mem_kv, w_mem_o=w_mem_o,
             w_ffn_up=w_ffn_up, ffn_conv_w=ffn_conv_w, ffn_conv_b=ffn_conv_b, w_ffn_down=w_ffn_down)
    w = _prep_weights(p)
    pos_p = jnp.arange(x_prompt.shape[1], dtype=jnp.int32)
    y_prompt, new_p = _trunk(x_prompt, pos_p, p, w, mem_prompt, None, None)
    past_len = page_table.shape[1] * cache_mla_ckv.shape[2]
    pos_s = past_len + jnp.arange(x_sample.shape[1], dtype=jnp.int32)
    past = dict(mla_ckv=cache_mla_ckv, mla_kpe=cache_mla_kpe, swa_k=cache_swa_k, swa_v=cache_swa_v,
                mem_k=cache_mem_k, mem_v=cache_mem_v, conf_conv=state_conf_conv, sc_conv=state_sc_conv,
                ffn_conv=state_ffn_conv, past_len=past_len)
    y_sample, new_s = _trunk(x_sample, pos_s, p, w, None, past, page_table)
    return (y_prompt, y_sample,
            new_p["mla_ckv"], new_p["mla_kpe"], new_p["swa_k"], new_p["swa_v"], new_p["mem_k"], new_p["mem_v"],
            new_p["conf_conv"], new_p["sc_conv"], new_p["ffn_conv"],
            new_s["mla_ckv"], new_s["mla_kpe"], new_s["swa_k"], new_s["swa_v"],
            new_s["conf_conv"], new_s["sc_conv"], new_s["ffn_conv"])
```
